```python
import math
import jax, jax.numpy as jnp
from jax import lax
import numpy as np

D_MODEL = 1024
BATCH = 4
SEQ = 8192
DEPTH = 2

N_MIXERS = 2
N_A_LAYERS = (DEPTH + 1) // 2
N_B_LAYERS = DEPTH // 2

CHUNK = 128
GATE_WIDTH = D_MODEL
GATE_GROUPS = 8
GATE_GROUP_DIM = GATE_WIDTH // GATE_GROUPS

WINDOW_DILATIONS = ((128, 1), (512, 4), (2048, 16))
N_DIL_GROUPS = len(WINDOW_DILATIONS)
ATT_HEADS = 8
HEAD_DIM = 64
ATT_WIDTH = ATT_HEADS * HEAD_DIM

N_BUCKETS = 32
MAX_EXACT = N_BUCKETS // 2
REL_MAX_DISTANCE = max(w for w, _ in WINDOW_DILATIONS)

D_FF = 4 * D_MODEL

EPS = 1e-6
NEG_INF = -1e30

kernel_name = "interleaved_gmlp_dilated_attention_trunk"


def _rms_norm(x, g):
    xf = x.astype(jnp.float32)
    y = xf * lax.rsqrt(jnp.mean(xf * xf, axis=-1, keepdims=True) + EPS)
    return (y * g.astype(jnp.float32)).astype(x.dtype)


def _layer_norm(x, g, b):
    xf = x.astype(jnp.float32)
    mu = jnp.mean(xf, axis=-1, keepdims=True)
    xc = xf - mu
    y = xc * lax.rsqrt(jnp.mean(xc * xc, axis=-1, keepdims=True) + EPS)
    return (y * g.astype(jnp.float32) + b.astype(jnp.float32)).astype(x.dtype)


def _t5_bucket(distance):
    small = distance < MAX_EXACT
    nf = jnp.maximum(distance, 1).astype(jnp.float32)
    large = MAX_EXACT + (jnp.log(nf / MAX_EXACT) / math.log(REL_MAX_DISTANCE / MAX_EXACT)
                         * (N_BUCKETS - MAX_EXACT)).astype(jnp.int32)
    large = jnp.minimum(large, N_BUCKETS - 1)
    return jnp.where(small, distance, large)


def _chunk_gating_mixer(h, w_in, ln_g, ln_b, w_s, b_s, w_out):
    B_, S_, _ = h.shape
    uv = jax.nn.gelu(h @ w_in, approximate=False)
    u, v = jnp.split(uv, 2, axis=-1)
    v = _layer_norm(v, ln_g, ln_b)
    nc = S_ // CHUNK
    v = v.reshape(B_, nc, CHUNK, GATE_GROUPS, GATE_GROUP_DIM)
    causal = jnp.tril(jnp.ones((CHUNK, CHUNK), dtype=bool))
    w = jnp.where(causal[None], w_s, 0.0)
    mixed = jnp.einsum('gts,bnsgc->bntgc', w, v) + b_s.T[None, None, :, :, None]
    gate = mixed.reshape(B_, S_, GATE_WIDTH)
    return (u * gate) @ w_out


def _dilated_group(q, k, v, bias_table, window, dilation):
    B_, S_, H, hd = q.shape
    blk = window // dilation
    span = blk * dilation
    Sp = -(-S_ // span) * span
    nb = Sp // span

    def split(t):
        t = jnp.pad(t, ((0, 0), (0, Sp - S_), (0, 0), (0, 0)))
        return t.reshape(B_, nb, blk, dilation, H, hd)

    def with_prev(t):
        prev = jnp.pad(t, ((0, 0), (1, 0), (0, 0), (0, 0), (0, 0), (0, 0)))[:, :-1]
        return jnp.concatenate([prev, t], axis=2)

    qb = split(q)
    kc = with_prev(split(k))
    vc = with_prev(split(v))

    s = jnp.einsum('bnqrhc,bnkrhc->bnrhqk', qb, kc) * (HEAD_DIM ** -0.5)
    rel = blk + jnp.arange(blk)[:, None] - jnp.arange(2 * blk)[None, :]
    band = (rel >= 0) & (rel <= blk)
    bucket = _t5_bucket(jnp.clip(rel, 0, blk) * dilation)
    bias = jnp.transpose(bias_table[bucket], (2, 0, 1))
    first = (jnp.arange(nb)[:, None, None] == 0) & (jnp.arange(2 * blk)[None, None, :] < blk)
    valid = band[None] & ~first
    logits = jnp.where(valid[None, :, None, None], s + bias[None, None, None], NEG_INF)

    m = jnp.max(logits, axis=-1)
    p = jnp.exp(logits - m[..., None])
    den = jnp.sum(p, axis=-1)
    num = jnp.einsum('bnrhqk,bnkrhc->bnqrhc', p, vc)

    num = num.reshape(B_, Sp, H, hd)[:, :S_]
    den = jnp.transpose(den, (0, 1, 4, 2, 3)).reshape(B_, Sp, H)[:, :S_]
    m = jnp.transpose(m, (0, 1, 4, 2, 3)).reshape(B_, Sp, H)[:, :S_]
    return num, den, m


def _dilated_attention_mixer(h, w_qkv, w_out, rel_bias):
    B_, S_, _ = h.shape
    qkv = (h @ w_qkv).astype(jnp.float32).reshape(B_, S_, 3, N_DIL_GROUPS, ATT_HEADS, HEAD_DIM)
    rb = rel_bias.astype(jnp.float32)
    nums, dens, maxs = [], [], []
    for g, (window, dil) in enumerate(WINDOW_DILATIONS):
        n_, d_, m_ = _dilated_group(qkv[:, :, 0, g], qkv[:, :, 1, g], qkv[:, :, 2, g],
                                    rb[:, g * ATT_HEADS:(g + 1) * ATT_HEADS], window, dil)
        nums.append(n_)
        dens.append(d_)
        maxs.append(m_)
    m_all = jnp.max(jnp.stack(maxs), axis=0)
    scales = [jnp.exp(m_ - m_all) for m_ in maxs]
    num_sum = sum(n_ * c[..., None] for n_, c in zip(nums, scales))
    den_sum = sum(d_ * c for d_, c in zip(dens, scales))
    o = (num_sum / den_sum[..., None]).astype(h.dtype).reshape(B_, S_, ATT_WIDTH)
    return o @ w_out


def setup_inputs(seed: int = 0) -> dict:
    key = jax.random.key(seed)
    ks = jax.random.split(key, 16)
    f32 = jnp.float32
    nrm = lambda k, shape, s: jax.random.normal(k, shape, f32) * s
    qkv_cols = 3 * N_DIL_GROUPS * ATT_HEADS * HEAD_DIM
    return {
        "x": jax.random.normal(ks[0], (BATCH, SEQ, D_MODEL), f32),
        "mix_norm_g": 1.0 + nrm(ks[1], (DEPTH, D_MODEL), 0.05),
        "mlp_norm_g": 1.0 + nrm(ks[2], (DEPTH, D_MODEL), 0.05),
        "final_norm_g": 1.0 + nrm(ks[3], (D_MODEL,), 0.05),
        "a_w_in": nrm(ks[4], (N_A_LAYERS, D_MODEL, 2 * GATE_WIDTH), D_MODEL ** -0.5),
        "a_ln_g": 1.0 + nrm(ks[5], (N_A_LAYERS, GATE_WIDTH), 0.05),
        "a_ln_b": nrm(ks[6], (N_A_LAYERS, GATE_WIDTH), 0.02),
        "a_w_s": nrm(ks[7], (N_A_LAYERS, GATE_GROUPS, CHUNK, CHUNK), CHUNK ** -0.5),
        "a_b_s": 1.0 + nrm(ks[8], (N_A_LAYERS, GATE_GROUPS, CHUNK), 0.1),
        "a_w_out": nrm(ks[9], (N_A_LAYERS, GATE_WIDTH, D_MODEL), GATE_WIDTH ** -0.5),
        "b_w_qkv": nrm(ks[10], (N_B_LAYERS, D_MODEL, qkv_cols), D_MODEL ** -0.5),
        "b_w_out": nrm(ks[11], (N_B_LAYERS, ATT_WIDTH, D_MODEL), ATT_WIDTH ** -0.5),
        "rel_bias": nrm(ks[12], (N_BUCKETS, N_DIL_GROUPS * ATT_HEADS), 0.5),
        "w_up": nrm(ks[13], (DEPTH, D_MODEL, D_FF), D_MODEL ** -0.5),
        "w_down": nrm(ks[14], (DEPTH, D_FF, D_MODEL), D_FF ** -0.5),
    }


def reference(x, mix_norm_g, mlp_norm_g, final_norm_g, a_w_in, a_ln_g, a_ln_b, a_w_s, a_b_s,
              a_w_out, b_w_qkv, b_w_out, rel_bias, w_up, w_down):
    h = x
    for layer in range(DEPTH):
        y = _rms_norm(h, mix_norm_g[layer])
        j = layer // N_MIXERS
        if layer % N_MIXERS == 0:
            y = _chunk_gating_mixer(y, a_w_in[j], a_ln_g[j], a_ln_b[j], a_w_s[j], a_b_s[j], a_w_out[j])
        else:
            y = _dilated_attention_mixer(y, b_w_qkv[j], b_w_out[j], rel_bias)
        h = h + y
        y = _rms_norm(h, mlp_norm_g[layer])
        h = h + jnp.square(jax.nn.relu(y @ w_up[layer])) @ w_down[layer]
    return _rms_norm(h, final_norm_g)
```

```python
import functools
import math

import jax
import jax.numpy as jnp
from jax import lax
from jax.experimental import pallas as pl
from jax.experimental.pallas import tpu as pltpu

D_MODEL = 1024
CHUNK = 128
GATE_WIDTH = D_MODEL
GATE_GROUPS = 8
WINDOW_DILATIONS = ((128, 1), (512, 4), (2048, 16))
N_DIL_GROUPS = len(WINDOW_DILATIONS)
ATT_HEADS = 8
HEAD_DIM = 64
ATT_WIDTH = ATT_HEADS * HEAD_DIM
N_BUCKETS = 32
MAX_EXACT = N_BUCKETS // 2
REL_MAX_DISTANCE = max(w for w, _ in WINDOW_DILATIONS)
D_FF = 4 * D_MODEL
EPS = 1e-6
NEG_INF = -1e30

BLK = 128
LANES = 128
TOKEN_TILE = 512
FF_CHUNK = 1024
VMEM_LIMIT = 56 * 1024 * 1024

F32 = jnp.float32
BF16 = jnp.bfloat16


def _dot(a, b):
    return jnp.dot(a, b, preferred_element_type=F32)


def _rms_norm(x, g):
    return x * lax.rsqrt(jnp.mean(x * x, axis=-1, keepdims=True) + EPS) * g


def _gelu_exact(x):
    return 0.5 * x * (1.0 + lax.erf(x * math.sqrt(0.5)))


def _mlp_residual(h, g, wup_ref, wdn_ref):
    xn = _rms_norm(h, g).astype(BF16)
    acc = h
    for c in range(D_FF // FF_CHUNK):
        cols = slice(c * FF_CHUNK, (c + 1) * FF_CHUNK)
        up = _dot(xn, wup_ref[:, cols])
        act = jnp.square(jnp.maximum(up, 0.0)).astype(BF16)
        acc = acc + _dot(act, wdn_ref[cols, :])
    return acc


def _layer0_kernel(h_ref, mixg_ref, win_ref, lng_ref, lnb_ref, ws_ref, bst_ref, wout_ref,
                   mlpg_ref, wup_ref, wdn_ref, out_ref):
    tm = h_ref.shape[0]
    nc = tm // CHUNK
    h = h_ref[...]
    xn = _rms_norm(h, mixg_ref[...]).astype(BF16)
    uv = _gelu_exact(_dot(xn, win_ref[...]))
    u = uv[:, :GATE_WIDTH]
    v = uv[:, GATE_WIDTH:]
    mu = jnp.mean(v, axis=-1, keepdims=True)
    vc = v - mu
    vn = vc * lax.rsqrt(jnp.mean(vc * vc, axis=-1, keepdims=True) + EPS)
    vb = (vn * lng_ref[...] + lnb_ref[...]).astype(BF16)

    t_idx = lax.broadcasted_iota(jnp.int32, (CHUNK, CHUNK), 0)
    s_idx = lax.broadcasted_iota(jnp.int32, (CHUNK, CHUNK), 1)
    causal = s_idx <= t_idx
    mixed = []
    for g in range(GATE_GROUPS):
        wg = jnp.where(causal, ws_ref[g], 0.0).astype(BF16)
        lanes = slice(g * CHUNK, (g + 1) * CHUNK)
        vg = jnp.concatenate([vb[c * CHUNK:(c + 1) * CHUNK, lanes] for c in range(nc)], axis=1)
        mixed.append(_dot(wg, vg) + bst_ref[:, g:g + 1])
    gate = jnp.concatenate(
        [jnp.concatenate([mixed[g][:, c * CHUNK:(c + 1) * CHUNK] for g in range(GATE_GROUPS)], axis=1)
         for c in range(nc)], axis=0)
    h1 = h + _dot((u * gate).astype(BF16), wout_ref[...])
    out_ref[...] = _mlp_residual(h1, mlpg_ref[...], wup_ref, wdn_ref)


def _whole(shape):
    return pl.BlockSpec(shape, lambda *_: (0,) * len(shape))


def _layer0(h, mix_g, w_in, ln_g, ln_b, w_s, b_st, w_out, mlp_g, w_up, w_down):
    t = h.shape[0]
    tile = pl.BlockSpec((TOKEN_TILE, D_MODEL), lambda i: (i, 0))
    return pl.pallas_call(
        _layer0_kernel,
        grid=(t // TOKEN_TILE,),
        in_specs=[tile, _whole((1, D_MODEL)), _whole(w_in.shape), _whole((1, GATE_WIDTH)),
                  _whole((1, GATE_WIDTH)), _whole(w_s.shape), _whole(b_st.shape), _whole(w_out.shape),
                  _whole((1, D_MODEL)), _whole(w_up.shape), _whole(w_down.shape)],
        out_specs=tile,
        out_shape=jax.ShapeDtypeStruct((t, D_MODEL), F32),
        compiler_params=pltpu.CompilerParams(dimension_semantics=("arbitrary",),
                                             vmem_limit_bytes=VMEM_LIMIT),
        name="layer0_gating_mlp",
    )(h, mix_g, w_in, ln_g, ln_b, w_s, b_st, w_out, mlp_g, w_up, w_down)


def _qkv_kernel(h_ref, g_ref, w_ref, *out_refs):
    xn = _rms_norm(h_ref[...], g_ref[...]).astype(BF16)
    for j, o_ref in enumerate(out_refs):
        y = _dot(xn, w_ref[:, j * ATT_WIDTH:(j + 1) * ATT_WIDTH])
        if j < N_DIL_GROUPS:
            y = y * (HEAD_DIM ** -0.5)
        o_ref[...] = y.astype(BF16)


def _qkv(h, g, w_qkv):
    t = h.shape[0]
    n_out = 3 * N_DIL_GROUPS
    tile = pl.BlockSpec((TOKEN_TILE, D_MODEL), lambda i: (i, 0))
    otile = pl.BlockSpec((TOKEN_TILE, ATT_WIDTH), lambda i: (i, 0))
    return pl.pallas_call(
        _qkv_kernel,
        grid=(t // TOKEN_TILE,),
        in_specs=[tile, _whole((1, D_MODEL)), _whole(w_qkv.shape)],
        out_specs=[otile] * n_out,
        out_shape=[jax.ShapeDtypeStruct((t, ATT_WIDTH), BF16)] * n_out,
        compiler_params=pltpu.CompilerParams(dimension_semantics=("arbitrary",),
                                             vmem_limit_bytes=VMEM_LIMIT),
        name="qkv_proj",
    )(h, g, w_qkv)


def _t5_bucket(distance):
    small = distance < MAX_EXACT
    nf = jnp.maximum(distance, 1).astype(F32)
    large = MAX_EXACT + (jnp.log(nf / MAX_EXACT) / math.log(REL_MAX_DISTANCE / MAX_EXACT)
                         * (N_BUCKETS - MAX_EXACT)).astype(jnp.int32)
    large = jnp.minimum(large, N_BUCKETS - 1)
    return jnp.where(small, distance, large)


def _bias_table_kernel(rb_ref, bucket_ref, out_ref):
    g = pl.program_id(0)
    h = pl.program_id(1)
    bucket = bucket_ref[...]
    acc = jnp.zeros((BLK, 2 * BLK), F32)
    for b in range(N_BUCKETS):
        acc = jnp.where(bucket == b, rb_ref[b, g * ATT_HEADS + h], acc)
    i_idx = lax.broadcasted_iota(jnp.int32, (BLK, 2 * BLK), 0)
    j_idx = lax.broadcasted_iota(jnp.int32, (BLK, 2 * BLK), 1)
    rel = BLK + i_idx - j_idx
    out_ref[...] = jnp.where((rel >= 0) & (rel <= BLK), acc, NEG_INF)


def _bias_tables(rel_bias):
    rel = BLK + jnp.arange(BLK)[:, None] - jnp.arange(2 * BLK)[None, :]
    buckets = jnp.stack([_t5_bucket(jnp.clip(rel, 0, BLK) * dil) for _, dil in WINDOW_DILATIONS])
    return pl.pallas_call(
        _bias_table_kernel,
        grid=(N_DIL_GROUPS, ATT_HEADS),
        in_specs=[pl.BlockSpec(memory_space=pltpu.SMEM),
                  pl.BlockSpec((None, BLK, 2 * BLK), lambda g, h: (g, 0, 0))],
        out_specs=pl.BlockSpec((None, None, BLK, 2 * BLK), lambda g, h: (g, h, 0, 0)),
        out_shape=jax.ShapeDtypeStruct((N_DIL_GROUPS, ATT_HEADS, BLK, 2 * BLK), F32),
        name="rel_bias_tables",
    )(rel_bias.astype(F32), buckets.astype(jnp.int32))


def _attn_kernel(q_ref, k_ref, v_ref, kp_ref, vp_ref, tb_ref, o_ref, lse_ref):
    rows = q_ref.shape[0]
    lane = lax.broadcasted_iota(jnp.int32, (BLK, LANES), 1)

    def unit(q, k2, v2, penalty):
        outs = []
        lse = jnp.zeros((BLK, LANES), F32)
        for h in range(ATT_HEADS):
            sl = slice(h * HEAD_DIM, (h + 1) * HEAD_DIM)
            s = lax.dot_general(q[:, sl], k2[:, sl], (((1,), (1,)), ((), ())),
                                preferred_element_type=F32)
            logits = s + tb_ref[h]
            if penalty is not None:
                logits = logits + penalty
            m = jnp.max(logits, axis=-1, keepdims=True)
            p = jnp.exp(logits - m)
            den = jnp.sum(p, axis=-1, keepdims=True)
            outs.append(_dot(p.astype(BF16), v2[:, sl]) / den)
            lse = jnp.where(lane == h, m + jnp.log(den), lse)
        return jnp.concatenate(outs, axis=1).astype(BF16), lse

    j_idx = lax.broadcasted_iota(jnp.int32, (BLK, 2 * BLK), 1)
    no_prev = pl.program_id(2) == 0
    penalty = jnp.where((j_idx < BLK) & no_prev, NEG_INF, 0.0)
    k2 = jnp.concatenate([kp_ref[...], k_ref[0:BLK, :]], axis=0)
    v2 = jnp.concatenate([vp_ref[...], v_ref[0:BLK, :]], axis=0)
    o, lse = unit(q_ref[0:BLK, :], k2, v2, penalty)
    o_ref[0:BLK, :] = o
    lse_ref[0:BLK, :] = lse

    def body(u, carry):
        r0 = pl.multiple_of(u * BLK, BLK)
        rp = pl.multiple_of(u * BLK - BLK, BLK)
        o, lse = unit(q_ref[pl.ds(r0, BLK), :], k_ref[pl.ds(rp, 2 * BLK), :],
                      v_ref[pl.ds(rp, 2 * BLK), :], None)
        o_ref[pl.ds(r0, BLK), :] = o
        lse_ref[pl.ds(r0, BLK), :] = lse
        return carry

    lax.fori_loop(1, rows // BLK, body, 0)


def _attention_group(q, k, v, table, batch, seq, dil, rows):
    sub = seq // dil
    w = ATT_WIDTH
    q, k, v = (a.reshape(batch, sub, dil * w) for a in (q, k, v))
    blocks_per_step = rows // BLK
    cur = pl.BlockSpec((None, rows, w), lambda b, r, i: (b, i, r))
    prev = pl.BlockSpec((None, BLK, w),
                        lambda b, r, i: (b, jnp.maximum(i * blocks_per_step - 1, 0), r))
    o, lse = pl.pallas_call(
        _attn_kernel,
        grid=(batch, dil, sub // rows),
        in_specs=[cur, cur, cur, prev, prev, _whole(table.shape)],
        out_specs=[cur, pl.BlockSpec((None, rows, LANES), lambda b, r, i: (b, i, r))],
        out_shape=[jax.ShapeDtypeStruct((batch, sub, dil * w), BF16),
                   jax.ShapeDtypeStruct((batch, sub, dil * LANES), F32)],
        compiler_params=pltpu.CompilerParams(
            dimension_semantics=("arbitrary", "arbitrary", "arbitrary"),
            vmem_limit_bytes=VMEM_LIMIT),
        name=f"dilated_attention_d{dil}",
    )(q, k, v, k, v, table)
    return o.reshape(batch * seq, w), lse.reshape(batch * seq, LANES)


def _layer1_tail_kernel(h_ref, o0_ref, o1_ref, o2_ref, l0_ref, l1_ref, l2_ref, wout_ref,
                        mlpg_ref, wup_ref, wdn_ref, fing_ref, out_ref):
    lses = [l0_ref[...], l1_ref[...], l2_ref[...]]
    m_all = jnp.maximum(jnp.maximum(lses[0], lses[1]), lses[2])
    es = [jnp.exp(l - m_all) for l in lses]
    den = es[0] + es[1] + es[2]
    row = lax.broadcasted_iota(jnp.int32, (LANES, ATT_WIDTH), 0)
    col = lax.broadcasted_iota(jnp.int32, (LANES, ATT_WIDTH), 1)
    expand = jnp.where(col // HEAD_DIM == row, 1.0, 0.0).astype(BF16)
    o = jnp.zeros(o0_ref.shape, F32)
    for e, o_ref in zip(es, (o0_ref, o1_ref, o2_ref)):
        wgt = e / den
        hi = wgt.astype(BF16)
        lo = (wgt - hi.astype(F32)).astype(BF16)
        o = o + (_dot(hi, expand) + _dot(lo, expand)) * o_ref[...].astype(F32)
    h1 = h_ref[...] + _dot(o.astype(BF16), wout_ref[...])
    h2 = _mlp_residual(h1, mlpg_ref[...], wup_ref, wdn_ref)
    out_ref[...] = _rms_norm(h2, fing_ref[...])


def _layer1_tail(h, os_, lses, w_out, mlp_g, w_up, w_down, fin_g):
    t = h.shape[0]
    tile = pl.BlockSpec((TOKEN_TILE, D_MODEL), lambda i: (i, 0))
    otile = pl.BlockSpec((TOKEN_TILE, ATT_WIDTH), lambda i: (i, 0))
    ltile = pl.BlockSpec((TOKEN_TILE, LANES), lambda i: (i, 0))
    return pl.pallas_call(
        _layer1_tail_kernel,
        grid=(t // TOKEN_TILE,),
        in_specs=[tile, otile, otile, otile, ltile, ltile, ltile, _whole(w_out.shape),
                  _whole((1, D_MODEL)), _whole(w_up.shape), _whole(w_down.shape),
                  _whole((1, D_MODEL))],
        out_specs=tile,
        out_shape=jax.ShapeDtypeStruct((t, D_MODEL), F32),
        compiler_params=pltpu.CompilerParams(dimension_semantics=("arbitrary",),
                                             vmem_limit_bytes=VMEM_LIMIT),
        name="layer1_merge_proj_mlp",
    )(h, *os_, *lses, w_out, mlp_g, w_up, w_down, fin_g)


def kernel(x, mix_norm_g, mlp_norm_g, final_norm_g, a_w_in, a_ln_g, a_ln_b, a_w_s, a_b_s, a_w_out,
           b_w_qkv, b_w_out, rel_bias, w_up, w_down):
    batch, seq, d = x.shape
    assert d == D_MODEL and seq % (BLK * WINDOW_DILATIONS[-1][1]) == 0
    assert (batch * seq) % TOKEN_TILE == 0 and TOKEN_TILE % CHUNK == 0 and seq % TOKEN_TILE == 0
    row = lambda a: a.reshape(1, -1).astype(F32)
    h = x.reshape(batch * seq, d)

    h = _layer0(h, row(mix_norm_g[0]), a_w_in[0].astype(BF16), row(a_ln_g[0]), row(a_ln_b[0]),
                a_w_s[0], a_b_s[0].T, a_w_out[0].astype(BF16), row(mlp_norm_g[0]),
                w_up[0].astype(BF16), w_down[0].astype(BF16))

    qkv = _qkv(h, row(mix_norm_g[1]), b_w_qkv[0].astype(BF16))
    tables = _bias_tables(rel_bias)
    os_, lses = [], []
    for g, (_, dil) in enumerate(WINDOW_DILATIONS):
        rows = min(2048, seq // dil)
        o, lse = _attention_group(qkv[g], qkv[N_DIL_GROUPS + g], qkv[2 * N_DIL_GROUPS + g],
                                  tables[g], batch, seq, dil, rows)
        os_.append(o)
        lses.append(lse)

    out = _layer1_tail(h, os_, lses, b_w_out[0].astype(BF16), row(mlp_norm_g[1]),
                       w_up[1].astype(BF16), w_down[1].astype(BF16), row(final_norm_g))
    return out.reshape(batch, seq, d)
```

```python
import math

import jax
import jax.numpy as jnp
from jax import lax
from jax.experimental import pallas as pl
from jax.experimental.pallas import tpu as pltpu

D_MODEL = 1024
CHUNK = 128
GATE_WIDTH = D_MODEL
GATE_GROUPS = 8
WINDOW_DILATIONS = ((128, 1), (512, 4), (2048, 16))
N_DIL_GROUPS = len(WINDOW_DILATIONS)
ATT_HEADS = 8
HEAD_DIM = 64
ATT_WIDTH = ATT_HEADS * HEAD_DIM
N_BUCKETS = 32
MAX_EXACT = N_BUCKETS // 2
REL_MAX_DISTANCE = max(w for w, _ in WINDOW_DILATIONS)
D_FF = 4 * D_MODEL
EPS = 1e-6
NEG_INF = -1e30

BLK = 128
LANES = 128
HEADS_PER_VREG = LANES // HEAD_DIM
HEAD_PAIRS = ATT_HEADS // HEADS_PER_VREG
TOKEN_TILE = 512
FF_CHUNK = 1024
VMEM_LIMIT = 56 * 1024 * 1024

F32 = jnp.float32
BF16 = jnp.bfloat16


def _dot(a, b):
    return jnp.dot(a, b, preferred_element_type=F32)


def _dot_nt(a, b):
    return lax.dot_general(a, b, (((1,), (1,)), ((), ())), preferred_element_type=F32)


def _rms_norm(x, g):
    return x * lax.rsqrt(jnp.mean(x * x, axis=-1, keepdims=True) + EPS) * g


def _gelu_exact(x):
    return 0.5 * x * (1.0 + lax.erf(x * math.sqrt(0.5)))


def _mlp_residual(h, g, wup_ref, wdn_ref):
    xn = _rms_norm(h, g).astype(BF16)
    acc = h
    for c in range(D_FF // FF_CHUNK):
        cols = slice(c * FF_CHUNK, (c + 1) * FF_CHUNK)
        up = _dot(xn, wup_ref[:, cols])
        act = jnp.square(jnp.maximum(up, 0.0)).astype(BF16)
        acc = acc + _dot(act, wdn_ref[cols, :])
    return acc


def _whole(shape):
    return pl.BlockSpec(shape, lambda *_: (0,) * len(shape))


def _layer0_kernel(h_ref, mixg_ref, win_ref, lng_ref, lnb_ref, ws_ref, bst_ref, wout_ref,
                   mlpg_ref, wup_ref, wdn_ref, out_ref):
    tm = h_ref.shape[0]
    nc = tm // CHUNK
    h = h_ref[...]
    xn = _rms_norm(h, mixg_ref[...]).astype(BF16)
    uv = _gelu_exact(_dot(xn, win_ref[...]))
    u = uv[:, :GATE_WIDTH]
    v = uv[:, GATE_WIDTH:]
    mu = jnp.mean(v, axis=-1, keepdims=True)
    vc = v - mu
    vn = vc * lax.rsqrt(jnp.mean(vc * vc, axis=-1, keepdims=True) + EPS)
    vb = (vn * lng_ref[...] + lnb_ref[...]).astype(BF16)

    t_idx = lax.broadcasted_iota(jnp.int32, (CHUNK, CHUNK), 0)
    s_idx = lax.broadcasted_iota(jnp.int32, (CHUNK, CHUNK), 1)
    causal = s_idx <= t_idx
    mixed = []
    for g in range(GATE_GROUPS):
        wg = jnp.where(causal, ws_ref[g], 0.0).astype(BF16)
        lanes = slice(g * CHUNK, (g + 1) * CHUNK)
        vg = jnp.concatenate([vb[c * CHUNK:(c + 1) * CHUNK, lanes] for c in range(nc)], axis=1)
        mixed.append(_dot(wg, vg) + bst_ref[:, g:g + 1])
    gate = jnp.concatenate(
        [jnp.concatenate([mixed[g][:, c * CHUNK:(c + 1) * CHUNK] for g in range(GATE_GROUPS)], axis=1)
         for c in range(nc)], axis=0)
    h1 = h + _dot((u * gate).astype(BF16), wout_ref[...])
    out_ref[...] = _mlp_residual(h1, mlpg_ref[...], wup_ref, wdn_ref)


def _layer0(h, mix_g, w_in, ln_g, ln_b, w_s, b_st, w_out, mlp_g, w_up, w_down):
    t = h.shape[0]
    tile = pl.BlockSpec((TOKEN_TILE, D_MODEL), lambda i: (i, 0))
    return pl.pallas_call(
        _layer0_kernel,
        grid=(t // TOKEN_TILE,),
        in_specs=[tile, _whole((1, D_MODEL)), _whole(w_in.shape), _whole((1, GATE_WIDTH)),
                  _whole((1, GATE_WIDTH)), _whole(w_s.shape), _whole(b_st.shape), _whole(w_out.shape),
                  _whole((1, D_MODEL)), _whole(w_up.shape), _whole(w_down.shape)],
        out_specs=tile,
        out_shape=jax.ShapeDtypeStruct((t, D_MODEL), F32),
        compiler_params=pltpu.CompilerParams(dimension_semantics=("arbitrary",),
                                             vmem_limit_bytes=VMEM_LIMIT),
        name="layer0_gating_mlp",
    )(h, mix_g, w_in, ln_g, ln_b, w_s, b_st, w_out, mlp_g, w_up, w_down)


def _qkv_kernel(h_ref, g_ref, w_ref, *refs):
    out_refs, xs_ref = refs[:-1], refs[-1]
    tm = h_ref.shape[0]
    n_col = D_MODEL // LANES
    xn = _rms_norm(h_ref[...], g_ref[...])
    for c in range(n_col):
        xs_ref[c] = xn[:, c * LANES:(c + 1) * LANES]
    for grp, (_, dil) in enumerate(WINDOW_DILATIONS):
        per = tm // dil
        if dil == 1:
            x = xn
        else:
            x = jnp.concatenate(
                [jnp.concatenate([xs_ref[c, pl.ds(r, per, stride=dil), :] for c in range(n_col)], axis=1)
                 for r in range(dil)], axis=0)
        x = x.astype(BF16)
        for part in range(3):
            j = part * N_DIL_GROUPS + grp
            y = _dot(x, w_ref[:, j * ATT_WIDTH:(j + 1) * ATT_WIDTH])
            if part == 0:
                y = y * (HEAD_DIM ** -0.5)
            y = y.astype(BF16)
            o_ref = out_refs[j]
            for r in range(dil):
                o_ref[:, r * ATT_WIDTH:(r + 1) * ATT_WIDTH] = y[r * per:(r + 1) * per, :]


def _qkv(h, g, w_qkv):
    t = h.shape[0]
    tile = pl.BlockSpec((TOKEN_TILE, D_MODEL), lambda i: (i, 0))
    out_specs, out_shape = [], []
    for _ in range(3):
        for _, dil in WINDOW_DILATIONS:
            out_specs.append(pl.BlockSpec((TOKEN_TILE // dil, dil * ATT_WIDTH), lambda i: (i, 0)))
            out_shape.append(jax.ShapeDtypeStruct((t // dil, dil * ATT_WIDTH), BF16))
    return pl.pallas_call(
        _qkv_kernel,
        grid=(t // TOKEN_TILE,),
        in_specs=[tile, _whole((1, D_MODEL)), _whole(w_qkv.shape)],
        out_specs=out_specs,
        out_shape=out_shape,
        scratch_shapes=[pltpu.VMEM((D_MODEL // LANES, TOKEN_TILE, LANES), F32)],
        compiler_params=pltpu.CompilerParams(dimension_semantics=("arbitrary",),
                                             vmem_limit_bytes=VMEM_LIMIT),
        name="qkv_proj",
    )(h, g, w_qkv)


def _t5_bucket(distance):
    small = distance < MAX_EXACT
    nf = jnp.maximum(distance, 1).astype(F32)
    large = MAX_EXACT + (jnp.log(nf / MAX_EXACT) / math.log(REL_MAX_DISTANCE / MAX_EXACT)
                         * (N_BUCKETS - MAX_EXACT)).astype(jnp.int32)
    large = jnp.minimum(large, N_BUCKETS - 1)
    return jnp.where(small, distance, large)


def _bias_table_kernel(rb_ref, bucket_ref, out_ref):
    g = pl.program_id(0)
    h = pl.program_id(1)
    bucket = bucket_ref[...]
    acc = jnp.zeros((BLK, 2 * BLK), F32)
    for b in range(N_BUCKETS):
        acc = jnp.where(bucket == b, rb_ref[b, g * ATT_HEADS + h], acc)
    i_idx = lax.broadcasted_iota(jnp.int32, (BLK, 2 * BLK), 0)
    j_idx = lax.broadcasted_iota(jnp.int32, (BLK, 2 * BLK), 1)
    rel = BLK + i_idx - j_idx
    out_ref[...] = jnp.where((rel >= 0) & (rel <= BLK), acc, NEG_INF)


def _bias_tables(rel_bias):
    rel = BLK + jnp.arange(BLK)[:, None] - jnp.arange(2 * BLK)[None, :]
    buckets = jnp.stack([_t5_bucket(jnp.clip(rel, 0, BLK) * dil) for _, dil in WINDOW_DILATIONS])
    return pl.pallas_call(
        _bias_table_kernel,
        grid=(N_DIL_GROUPS, ATT_HEADS),
        in_specs=[pl.BlockSpec(memory_space=pltpu.SMEM),
                  pl.BlockSpec((None, BLK, 2 * BLK), lambda g, h: (g, 0, 0))],
        out_specs=pl.BlockSpec((None, None, BLK, 2 * BLK), lambda g, h: (g, h, 0, 0)),
        out_shape=jax.ShapeDtypeStruct((N_DIL_GROUPS, ATT_HEADS, BLK, 2 * BLK), F32),
        name="rel_bias_tables",
    )(rel_bias.astype(F32), buckets.astype(jnp.int32))


def _lse_lane(head):
    return (head % HEADS_PER_VREG) * HEAD_DIM + HEADS_PER_VREG * (head // HEADS_PER_VREG)


def _attn_kernel(q_ref, k_ref, v_ref, kp_ref, vp_ref, tb_ref, o_ref, lse_ref):
    rows = q_ref.shape[0]

    def unit(q, k2, v2, penalty):
        lane = lax.broadcasted_iota(jnp.int32, (BLK, LANES), 1)
        lane_kv = lax.broadcasted_iota(jnp.int32, (2 * BLK, LANES), 1)
        first = lane < HEAD_DIM
        first_kv = lane_kv < HEAD_DIM
        outs = []
        lse = jnp.zeros((BLK, LANES), F32)
        for pair in range(HEAD_PAIRS):
            sl = slice(pair * LANES, (pair + 1) * LANES)
            qp, kp, vp = q[:, sl], k2[:, sl], v2[:, sl]
            probs, maxes = [], []
            for half in range(HEADS_PER_VREG):
                mine = first if half == 0 else jnp.logical_not(first)
                logits = _dot_nt(jnp.where(mine, qp, jnp.zeros_like(qp)), kp)
                logits = logits + tb_ref[pair * HEADS_PER_VREG + half]
                if penalty is not None:
                    logits = logits + penalty
                m = jnp.max(logits, axis=-1, keepdims=True)
                probs.append(jnp.exp(logits - m).astype(BF16))
                maxes.append(m)
            v_ext = []
            for half in range(HEADS_PER_VREG):
                mine = first_kv if half == 0 else jnp.logical_not(first_kv)
                v_ext.append(jnp.concatenate(
                    [jnp.where(mine, vp, jnp.zeros_like(vp)), mine.astype(BF16)], axis=1))
            res = _dot(jnp.concatenate(probs, axis=1), jnp.concatenate(v_ext, axis=0))
            num, den = res[:, :LANES], res[:, LANES:]
            outs.append((num / den).astype(BF16))
            lse_pair = jnp.where(first, maxes[0], maxes[1]) + jnp.log(den)
            keep = (lane == _lse_lane(pair * HEADS_PER_VREG)) | (lane == _lse_lane(pair * HEADS_PER_VREG + 1))
            lse = jnp.where(keep, lse_pair, lse)
        return jnp.concatenate(outs, axis=1), lse

    j_idx = lax.broadcasted_iota(jnp.int32, (BLK, 2 * BLK), 1)
    no_prev = pl.program_id(2) == 0
    penalty = jnp.where((j_idx < BLK) & no_prev, NEG_INF, 0.0)
    k2 = jnp.concatenate([kp_ref[...], k_ref[0:BLK, :]], axis=0)
    v2 = jnp.concatenate([vp_ref[...], v_ref[0:BLK, :]], axis=0)
    o, lse = unit(q_ref[0:BLK, :], k2, v2, penalty)
    o_ref[0:BLK, :] = o
    lse_ref[0:BLK, :] = lse

    def body(u, carry):
        r0 = pl.multiple_of(u * BLK, BLK)
        rp = pl.multiple_of(u * BLK - BLK, BLK)
        o, lse = unit(q_ref[pl.ds(r0, BLK), :], k_ref[pl.ds(rp, 2 * BLK), :],
                      v_ref[pl.ds(rp, 2 * BLK), :], None)
        o_ref[pl.ds(r0, BLK), :] = o
        lse_ref[pl.ds(r0, BLK), :] = lse
        return carry

    lax.fori_loop(1, rows // BLK, body, 0)


def _attention_group(q, k, v, table, batch, seq, dil, rows):
    sub = seq // dil
    w = ATT_WIDTH
    q, k, v = (a.reshape(batch, sub, dil * w) for a in (q, k, v))
    blocks_per_step = rows // BLK
    cur = pl.BlockSpec((None, rows, w), lambda b, r, i: (b, i, r))
    prev = pl.BlockSpec((None, BLK, w),
                        lambda b, r, i: (b, jnp.maximum(i * blocks_per_step - 1, 0), r))
    o, lse = pl.pallas_call(
        _attn_kernel,
        grid=(batch, dil, sub // rows),
        in_specs=[cur, cur, cur, prev, prev, _whole(table.shape)],
        out_specs=[cur, pl.BlockSpec((None, rows, LANES), lambda b, r, i: (b, i, r))],
        out_shape=[jax.ShapeDtypeStruct((batch, sub, dil * w), BF16),
                   jax.ShapeDtypeStruct((batch, sub, dil * LANES), F32)],
        compiler_params=pltpu.CompilerParams(
            dimension_semantics=("arbitrary", "arbitrary", "arbitrary"),
            vmem_limit_bytes=VMEM_LIMIT),
        name=f"dilated_attention_d{dil}",
    )(q, k, v, k, v, table)
    return o.reshape(batch * sub, dil * w), lse.reshape(batch * sub, dil * LANES)


def _natural_order(ref, scr_ref, dil, width):
    if dil == 1:
        return ref[...].astype(F32)
    per = ref.shape[0]
    n_col = width // LANES
    for r in range(dil):
        for c in range(n_col):
            lanes = slice(r * width + c * LANES, r * width + (c + 1) * LANES)
            scr_ref[c, pl.ds(r, per, stride=dil), :] = ref[:, lanes].astype(F32)
    return jnp.concatenate([scr_ref[c] for c in range(n_col)], axis=1)


def _layer1_tail_kernel(h_ref, o0_ref, o1_ref, o2_ref, l0_ref, l1_ref, l2_ref, wout_ref,
                        mlpg_ref, wup_ref, wdn_ref, fing_ref, out_ref,
                        os1_ref, os2_ref, ls1_ref, ls2_ref):
    dils = [d for _, d in WINDOW_DILATIONS]
    lses = [_natural_order(l0_ref, None, dils[0], LANES),
            _natural_order(l1_ref, ls1_ref, dils[1], LANES),
            _natural_order(l2_ref, ls2_ref, dils[2], LANES)]
    outs = [_natural_order(o0_ref, None, dils[0], ATT_WIDTH),
            _natural_order(o1_ref, os1_ref, dils[1], ATT_WIDTH),
            _natural_order(o2_ref, os2_ref, dils[2], ATT_WIDTH)]
    m_all = jnp.maximum(jnp.maximum(lses[0], lses[1]), lses[2])
    es = [jnp.exp(l - m_all) for l in lses]
    den = es[0] + es[1] + es[2]
    row = lax.broadcasted_iota(jnp.int32, (LANES, ATT_WIDTH), 0)
    head = lax.broadcasted_iota(jnp.int32, (LANES, ATT_WIDTH), 1) // HEAD_DIM
    src = (head % HEADS_PER_VREG) * HEAD_DIM + HEADS_PER_VREG * (head // HEADS_PER_VREG)
    expand = jnp.where(row == src, 1.0, 0.0).astype(BF16)
    o = jnp.zeros(outs[0].shape, F32)
    for e, og in zip(es, outs):
        wgt = e / den
        hi = wgt.astype(BF16)
        lo = (wgt - hi.astype(F32)).astype(BF16)
        o = o + (_dot(hi, expand) + _dot(lo, expand)) * og
    h1 = h_ref[...] + _dot(o.astype(BF16), wout_ref[...])
    h2 = _mlp_residual(h1, mlpg_ref[...], wup_ref, wdn_ref)
    out_ref[...] = _rms_norm(h2, fing_ref[...])


def _layer1_tail(h, os_, lses, w_out, mlp_g, w_up, w_down, fin_g):
    t = h.shape[0]
    tile = pl.BlockSpec((TOKEN_TILE, D_MODEL), lambda i: (i, 0))
    otiles = [pl.BlockSpec((TOKEN_TILE // d, d * ATT_WIDTH), lambda i: (i, 0)) for _, d in WINDOW_DILATIONS]
    ltiles = [pl.BlockSpec((TOKEN_TILE // d, d * LANES), lambda i: (i, 0)) for _, d in WINDOW_DILATIONS]
    return pl.pallas_call(
        _layer1_tail_kernel,
        grid=(t // TOKEN_TILE,),
        in_specs=[tile, *otiles, *ltiles, _whole(w_out.shape),
                  _whole((1, D_MODEL)), _whole(w_up.shape), _whole(w_down.shape),
                  _whole((1, D_MODEL))],
        out_specs=tile,
        out_shape=jax.ShapeDtypeStruct((t, D_MODEL), F32),
        scratch_shapes=[pltpu.VMEM((ATT_WIDTH // LANES, TOKEN_TILE, LANES), F32),
                        pltpu.VMEM((ATT_WIDTH // LANES, TOKEN_TILE, LANES), F32),
                        pltpu.VMEM((1, TOKEN_TILE, LANES), F32), pltpu.VMEM((1, TOKEN_TILE, LANES), F32)],
        compiler_params=pltpu.CompilerParams(dimension_semantics=("arbitrary",),
                                             vmem_limit_bytes=VMEM_LIMIT),
        name="layer1_merge_proj_mlp",
    )(h, *os_, *lses, w_out, mlp_g, w_up, w_down, fin_g)


def kernel(x, mix_norm_g, mlp_norm_g, final_norm_g, a_w_in, a_ln_g, a_ln_b, a_w_s, a_b_s, a_w_out,
           b_w_qkv, b_w_out, rel_bias, w_up, w_down):
    batch, seq, d = x.shape
    max_dil = WINDOW_DILATIONS[-1][1]
    assert d == D_MODEL and seq % (BLK * max_dil) == 0
    assert seq % TOKEN_TILE == 0 and TOKEN_TILE % CHUNK == 0 and TOKEN_TILE % (16 * max_dil) == 0
    row = lambda a: a.reshape(1, -1).astype(F32)
    h = x.reshape(batch * seq, d)

    h = _layer0(h, row(mix_norm_g[0]), a_w_in[0].astype(BF16), row(a_ln_g[0]), row(a_ln_b[0]),
                a_w_s[0], a_b_s[0].T, a_w_out[0].astype(BF16), row(mlp_norm_g[0]),
                w_up[0].astype(BF16), w_down[0].astype(BF16))

    qkv = _qkv(h, row(mix_norm_g[1]), b_w_qkv[0].astype(BF16))
    tables = _bias_tables(rel_bias)
    os_, lses = [], []
    for g, (_, dil) in enumerate(WINDOW_DILATIONS):
        rows = min(2048, seq // dil)
        o, lse = _attention_group(qkv[g], qkv[N_DIL_GROUPS + g], qkv[2 * N_DIL_GROUPS + g],
                                  tables[g], batch, seq, dil, rows)
        os_.append(o)
        lses.append(lse)

    out = _layer1_tail(h, os_, lses, b_w_out[0].astype(BF16), row(mlp_norm_g[1]),
                       w_up[1].astype(BF16), w_down[1].astype(BF16), row(final_norm_g))
    return out.reshape(batch, seq, d)
```

```python
import math

import jax
import jax.numpy as jnp
from jax import lax
from jax.experimental import pallas as pl
from jax.experimental.pallas import tpu as pltpu

D_MODEL = 1024
CHUNK = 128
GATE_WIDTH = D_MODEL
GATE_GROUPS = 8
WINDOW_DILATIONS = ((128, 1), (512, 4), (2048, 16))
N_DIL_GROUPS = len(WINDOW_DILATIONS)
ATT_HEADS = 8
HEAD_DIM = 64
ATT_WIDTH = ATT_HEADS * HEAD_DIM
N_BUCKETS = 32
MAX_EXACT = N_BUCKETS // 2
REL_MAX_DISTANCE = max(w for w, _ in WINDOW_DILATIONS)
D_FF = 4 * D_MODEL
EPS = 1e-6
NEG_INF = -1e30
LOG2E = math.log2(math.e)

BLK = 128
LANES = 128
HEADS_PER_VREG = LANES // HEAD_DIM
HEAD_PAIRS = ATT_HEADS // HEADS_PER_VREG
TOKEN_TILE = 512
FF_CHUNK = 1024
ATTN_UNROLL = 8
VMEM_LIMIT = 56 * 1024 * 1024

F32 = jnp.float32
BF16 = jnp.bfloat16


def _dot(a, b):
    return jnp.dot(a, b, preferred_element_type=F32)


def _dot_nt(a, b):
    return lax.dot_general(a, b, (((1,), (1,)), ((), ())), preferred_element_type=F32)


def _rms_norm(x, g):
    return x * lax.rsqrt(jnp.mean(x * x, axis=-1, keepdims=True) + EPS) * g


def _gelu_exact(x):
    return 0.5 * x * (1.0 + lax.erf(x * math.sqrt(0.5)))


def _mlp_residual(h, g, wup_ref, wdn_ref):
    xn = _rms_norm(h, g).astype(BF16)
    acc = h
    for c in range(D_FF // FF_CHUNK):
        cols = slice(c * FF_CHUNK, (c + 1) * FF_CHUNK)
        up = _dot(xn, wup_ref[:, cols])
        act = jnp.square(jnp.maximum(up, 0.0)).astype(BF16)
        acc = acc + _dot(act, wdn_ref[cols, :])
    return acc


def _whole(shape):
    return pl.BlockSpec(shape, lambda *_: (0,) * len(shape))


def _layer0_kernel(h_ref, mixg_ref, win_ref, lng_ref, lnb_ref, ws_ref, bst_ref, wout_ref,
                   mlpg_ref, wup_ref, wdn_ref, out_ref):
    tm = h_ref.shape[0]
    nc = tm // CHUNK
    h = h_ref[...]
    xn = _rms_norm(h, mixg_ref[...]).astype(BF16)
    uv = _gelu_exact(_dot(xn, win_ref[...]))
    u = uv[:, :GATE_WIDTH]
    v = uv[:, GATE_WIDTH:]
    mu = jnp.mean(v, axis=-1, keepdims=True)
    vc = v - mu
    vn = vc * lax.rsqrt(jnp.mean(vc * vc, axis=-1, keepdims=True) + EPS)
    vb = (vn * lng_ref[...] + lnb_ref[...]).astype(BF16)

    t_idx = lax.broadcasted_iota(jnp.int32, (CHUNK, CHUNK), 0)
    s_idx = lax.broadcasted_iota(jnp.int32, (CHUNK, CHUNK), 1)
    causal = s_idx <= t_idx
    mixed = []
    for g in range(GATE_GROUPS):
        wg = jnp.where(causal, ws_ref[g], 0.0).astype(BF16)
        lanes = slice(g * CHUNK, (g + 1) * CHUNK)
        vg = jnp.concatenate([vb[c * CHUNK:(c + 1) * CHUNK, lanes] for c in range(nc)], axis=1)
        mixed.append(_dot(wg, vg) + bst_ref[:, g:g + 1])
    gate = jnp.concatenate(
        [jnp.concatenate([mixed[g][:, c * CHUNK:(c + 1) * CHUNK] for g in range(GATE_GROUPS)], axis=1)
         for c in range(nc)], axis=0)
    h1 = h + _dot((u * gate).astype(BF16), wout_ref[...])
    out_ref[...] = _mlp_residual(h1, mlpg_ref[...], wup_ref, wdn_ref)


def _layer0(h, mix_g, w_in, ln_g, ln_b, w_s, b_st, w_out, mlp_g, w_up, w_down):
    t = h.shape[0]
    tile = pl.BlockSpec((TOKEN_TILE, D_MODEL), lambda i: (i, 0))
    return pl.pallas_call(
        _layer0_kernel,
        grid=(t // TOKEN_TILE,),
        in_specs=[tile, _whole((1, D_MODEL)), _whole(w_in.shape), _whole((1, GATE_WIDTH)),
                  _whole((1, GATE_WIDTH)), _whole(w_s.shape), _whole(b_st.shape), _whole(w_out.shape),
                  _whole((1, D_MODEL)), _whole(w_up.shape), _whole(w_down.shape)],
        out_specs=tile,
        out_shape=jax.ShapeDtypeStruct((t, D_MODEL), F32),
        compiler_params=pltpu.CompilerParams(dimension_semantics=("arbitrary",),
                                             vmem_limit_bytes=VMEM_LIMIT),
        name="layer0_gating_mlp",
    )(h, mix_g, w_in, ln_g, ln_b, w_s, b_st, w_out, mlp_g, w_up, w_down)


def _qkv_kernel(h_ref, g_ref, w_ref, *refs):
    out_refs, xs_ref = refs[:-1], refs[-1]
    tm = h_ref.shape[0]
    n_col = D_MODEL // LANES
    xn = _rms_norm(h_ref[...], g_ref[...])
    for c in range(n_col):
        xs_ref[c] = xn[:, c * LANES:(c + 1) * LANES]
    for grp, (_, dil) in enumerate(WINDOW_DILATIONS):
        per = tm // dil
        if dil == 1:
            x = xn
        else:
            x = jnp.concatenate(
                [jnp.concatenate([xs_ref[c, pl.ds(r, per, stride=dil), :] for c in range(n_col)], axis=1)
                 for r in range(dil)], axis=0)
        x = x.astype(BF16)
        for part in range(3):
            j = part * N_DIL_GROUPS + grp
            y = _dot(x, w_ref[:, j * ATT_WIDTH:(j + 1) * ATT_WIDTH])
            if part == 0:
                y = y * (HEAD_DIM ** -0.5 * LOG2E)
            y = y.astype(BF16)
            o_ref = out_refs[j]
            for r in range(dil):
                o_ref[:, r * ATT_WIDTH:(r + 1) * ATT_WIDTH] = y[r * per:(r + 1) * per, :]


def _qkv(h, g, w_qkv):
    t = h.shape[0]
    tile = pl.BlockSpec((TOKEN_TILE, D_MODEL), lambda i: (i, 0))
    out_specs, out_shape = [], []
    for _ in range(3):
        for _, dil in WINDOW_DILATIONS:
            out_specs.append(pl.BlockSpec((TOKEN_TILE // dil, dil * ATT_WIDTH), lambda i: (i, 0)))
            out_shape.append(jax.ShapeDtypeStruct((t // dil, dil * ATT_WIDTH), BF16))
    return pl.pallas_call(
        _qkv_kernel,
        grid=(t // TOKEN_TILE,),
        in_specs=[tile, _whole((1, D_MODEL)), _whole(w_qkv.shape)],
        out_specs=out_specs,
        out_shape=out_shape,
        scratch_shapes=[pltpu.VMEM((D_MODEL // LANES, TOKEN_TILE, LANES), F32)],
        compiler_params=pltpu.CompilerParams(dimension_semantics=("arbitrary",),
                                             vmem_limit_bytes=VMEM_LIMIT),
        name="qkv_proj",
    )(h, g, w_qkv)


def _t5_bucket(distance):
    small = distance < MAX_EXACT
    nf = jnp.maximum(distance, 1).astype(F32)
    large = MAX_EXACT + (jnp.log(nf / MAX_EXACT) / math.log(REL_MAX_DISTANCE / MAX_EXACT)
                         * (N_BUCKETS - MAX_EXACT)).astype(jnp.int32)
    large = jnp.minimum(large, N_BUCKETS - 1)
    return jnp.where(small, distance, large)


def _bias_table_kernel(rb_ref, bucket_ref, out_ref):
    g = pl.program_id(0)
    h = pl.program_id(1)
    bucket = bucket_ref[...]
    acc = jnp.zeros((BLK, 2 * BLK), F32)
    for b in range(N_BUCKETS):
        acc = jnp.where(bucket == b, rb_ref[b, g * ATT_HEADS + h], acc)
    i_idx = lax.broadcasted_iota(jnp.int32, (BLK, 2 * BLK), 0)
    j_idx = lax.broadcasted_iota(jnp.int32, (BLK, 2 * BLK), 1)
    rel = BLK + i_idx - j_idx
    out_ref[...] = jnp.where((rel >= 0) & (rel <= BLK), acc * LOG2E, NEG_INF)


def _bias_tables(rel_bias):
    rel = BLK + jnp.arange(BLK)[:, None] - jnp.arange(2 * BLK)[None, :]
    buckets = jnp.stack([_t5_bucket(jnp.clip(rel, 0, BLK) * dil) for _, dil in WINDOW_DILATIONS])
    return pl.pallas_call(
        _bias_table_kernel,
        grid=(N_DIL_GROUPS, ATT_HEADS),
        in_specs=[pl.BlockSpec(memory_space=pltpu.SMEM),
                  pl.BlockSpec((None, BLK, 2 * BLK), lambda g, h: (g, 0, 0))],
        out_specs=pl.BlockSpec((None, None, BLK, 2 * BLK),
                               lambda g, h: (g, h // HEADS_PER_VREG, h % HEADS_PER_VREG, 0)),
        out_shape=jax.ShapeDtypeStruct((N_DIL_GROUPS, HEAD_PAIRS, HEADS_PER_VREG * BLK, 2 * BLK), F32),
        name="rel_bias_tables",
    )(rel_bias.astype(F32), buckets.astype(jnp.int32))


def _lse_lane(head):
    return (head % HEADS_PER_VREG) * HEAD_DIM + HEADS_PER_VREG * (head // HEADS_PER_VREG)


def _attn_kernel(q_ref, k_ref, v_ref, kp_ref, vp_ref, tb_ref, o_ref, lse_ref):
    rows = q_ref.shape[0]

    def unit(q, k2, v2, penalty):
        lane = lax.broadcasted_iota(jnp.int32, (BLK, LANES), 1)
        first = lane < HEAD_DIM
        ones = jnp.ones((2 * BLK, LANES), BF16)
        outs = []
        lse = jnp.zeros((BLK, LANES), F32)
        for pair in range(HEAD_PAIRS):
            sl = slice(pair * LANES, (pair + 1) * LANES)
            qp, kp, vp = q[:, sl], k2[:, sl], v2[:, sl]
            zero = jnp.zeros_like(qp)
            qq = jnp.concatenate([jnp.where(first, qp, zero), jnp.where(first, zero, qp)], axis=0)
            logits = _dot_nt(qq, kp) + tb_ref[pair]
            if penalty is not None:
                logits = logits + penalty
            m = jnp.max(logits, axis=-1, keepdims=True)
            p = jnp.exp2(logits - m).astype(BF16)
            res = _dot(p, jnp.concatenate([vp, ones], axis=1))
            num = jnp.where(first, res[:BLK, :LANES], res[BLK:, :LANES])
            den = jnp.where(first, res[:BLK, LANES:], res[BLK:, LANES:])
            outs.append((num / den).astype(BF16))
            lse_pair = jnp.where(first, m[:BLK], m[BLK:]) + jnp.log2(den)
            keep = (lane == _lse_lane(pair * HEADS_PER_VREG)) | (lane == _lse_lane(pair * HEADS_PER_VREG + 1))
            lse = jnp.where(keep, lse_pair, lse)
        return jnp.concatenate(outs, axis=1), lse

    j_idx = lax.broadcasted_iota(jnp.int32, (2 * BLK, 2 * BLK), 1)
    no_prev = pl.program_id(2) == 0
    penalty = jnp.where((j_idx < BLK) & no_prev, NEG_INF, 0.0)
    k2 = jnp.concatenate([kp_ref[...], k_ref[0:BLK, :]], axis=0)
    v2 = jnp.concatenate([vp_ref[...], v_ref[0:BLK, :]], axis=0)
    o, lse = unit(q_ref[0:BLK, :], k2, v2, penalty)
    o_ref[0:BLK, :] = o
    lse_ref[0:BLK, :] = lse

    def later_unit(r0):
        rp = r0 - BLK
        o, lse = unit(q_ref[pl.ds(r0, BLK), :], k_ref[pl.ds(rp, 2 * BLK), :],
                      v_ref[pl.ds(rp, 2 * BLK), :], None)
        o_ref[pl.ds(r0, BLK), :] = o
        lse_ref[pl.ds(r0, BLK), :] = lse

    unroll = min(ATTN_UNROLL, rows // BLK)
    for j in range(1, unroll):
        later_unit(j * BLK)

    def body(i, carry):
        r0 = pl.multiple_of(i * (unroll * BLK), unroll * BLK)
        for j in range(unroll):
            later_unit(r0 + j * BLK)
        return carry

    lax.fori_loop(1, rows // (unroll * BLK), body, 0)


def _attention_group(q, k, v, table, batch, seq, dil, rows):
    sub = seq // dil
    w = ATT_WIDTH
    q, k, v = (a.reshape(batch, sub, dil * w) for a in (q, k, v))
    blocks_per_step = rows // BLK
    cur = pl.BlockSpec((None, rows, w), lambda b, r, i: (b, i, r))
    prev = pl.BlockSpec((None, BLK, w),
                        lambda b, r, i: (b, jnp.maximum(i * blocks_per_step - 1, 0), r))
    o, lse = pl.pallas_call(
        _attn_kernel,
        grid=(batch, dil, sub // rows),
        in_specs=[cur, cur, cur, prev, prev, _whole(table.shape)],
        out_specs=[cur, pl.BlockSpec((None, rows, LANES), lambda b, r, i: (b, i, r))],
        out_shape=[jax.ShapeDtypeStruct((batch, sub, dil * w), BF16),
                   jax.ShapeDtypeStruct((batch, sub, dil * LANES), F32)],
        compiler_params=pltpu.CompilerParams(
            dimension_semantics=("arbitrary", "arbitrary", "arbitrary"),
            vmem_limit_bytes=VMEM_LIMIT),
        name=f"dilated_attention_d{dil}",
    )(q, k, v, k, v, table)
    return o.reshape(batch * sub, dil * w), lse.reshape(batch * sub, dil * LANES)


def _natural_order(ref, scr_ref, dil, width):
    if dil == 1:
        return ref[...].astype(F32)
    per = ref.shape[0]
    n_col = width // LANES
    for r in range(dil):
        for c in range(n_col):
            lanes = slice(r * width + c * LANES, r * width + (c + 1) * LANES)
            scr_ref[c, pl.ds(r, per, stride=dil), :] = ref[:, lanes].astype(F32)
    return jnp.concatenate([scr_ref[c] for c in range(n_col)], axis=1)


def _layer1_tail_kernel(h_ref, o0_ref, o1_ref, o2_ref, l0_ref, l1_ref, l2_ref, wout_ref,
                        mlpg_ref, wup_ref, wdn_ref, fing_ref, out_ref,
                        os1_ref, os2_ref, ls1_ref, ls2_ref):
    dils = [d for _, d in WINDOW_DILATIONS]
    lses = [_natural_order(l0_ref, None, dils[0], LANES),
            _natural_order(l1_ref, ls1_ref, dils[1], LANES),
            _natural_order(l2_ref, ls2_ref, dils[2], LANES)]
    outs = [_natural_order(o0_ref, None, dils[0], ATT_WIDTH),
            _natural_order(o1_ref, os1_ref, dils[1], ATT_WIDTH),
            _natural_order(o2_ref, os2_ref, dils[2], ATT_WIDTH)]
    m_all = jnp.maximum(jnp.maximum(lses[0], lses[1]), lses[2])
    es = [jnp.exp2(l - m_all) for l in lses]
    row = lax.broadcasted_iota(jnp.int32, (LANES, ATT_WIDTH), 0)
    head = lax.broadcasted_iota(jnp.int32, (LANES, ATT_WIDTH), 1) // HEAD_DIM
    src = (head % HEADS_PER_VREG) * HEAD_DIM + HEADS_PER_VREG * (head // HEADS_PER_VREG)
    expand = jnp.where(row == src, 1.0, 0.0).astype(BF16)
    num = jnp.zeros(outs[0].shape, F32)
    den = jnp.zeros(outs[0].shape, F32)
    for e, og in zip(es, outs):
        wide = _dot(e.astype(BF16), expand)
        num = num + wide * og
        den = den + wide
    h1 = h_ref[...] + _dot((num / den).astype(BF16), wout_ref[...])
    h2 = _mlp_residual(h1, mlpg_ref[...], wup_ref, wdn_ref)
    out_ref[...] = _rms_norm(h2, fing_ref[...])


def _layer1_tail(h, os_, lses, w_out, mlp_g, w_up, w_down, fin_g):
    t = h.shape[0]
    tile = pl.BlockSpec((TOKEN_TILE, D_MODEL), lambda i: (i, 0))
    otiles = [pl.BlockSpec((TOKEN_TILE // d, d * ATT_WIDTH), lambda i: (i, 0)) for _, d in WINDOW_DILATIONS]
    ltiles = [pl.BlockSpec((TOKEN_TILE // d, d * LANES), lambda i: (i, 0)) for _, d in WINDOW_DILATIONS]
    return pl.pallas_call(
        _layer1_tail_kernel,
        grid=(t // TOKEN_TILE,),
        in_specs=[tile, *otiles, *ltiles, _whole(w_out.shape),
                  _whole((1, D_MODEL)), _whole(w_up.shape), _whole(w_down.shape),
                  _whole((1, D_MODEL))],
        out_specs=tile,
        out_shape=jax.ShapeDtypeStruct((t, D_MODEL), F32),
        scratch_shapes=[pltpu.VMEM((ATT_WIDTH // LANES, TOKEN_TILE, LANES), F32),
                        pltpu.VMEM((ATT_WIDTH // LANES, TOKEN_TILE, LANES), F32),
                        pltpu.VMEM((1, TOKEN_TILE, LANES), F32), pltpu.VMEM((1, TOKEN_TILE, LANES), F32)],
        compiler_params=pltpu.CompilerParams(dimension_semantics=("arbitrary",),
                                             vmem_limit_bytes=VMEM_LIMIT),
        name="layer1_merge_proj_mlp",
    )(h, *os_, *lses, w_out, mlp_g, w_up, w_down, fin_g)


def kernel(x, mix_norm_g, mlp_norm_g, final_norm_g, a_w_in, a_ln_g, a_ln_b, a_w_s, a_b_s, a_w_out,
           b_w_qkv, b_w_out, rel_bias, w_up, w_down):
    batch, seq, d = x.shape
    max_dil = WINDOW_DILATIONS[-1][1]
    assert d == D_MODEL and seq % (BLK * max_dil) == 0
    assert seq % TOKEN_TILE == 0 and TOKEN_TILE % CHUNK == 0 and TOKEN_TILE % (16 * max_dil) == 0
    row = lambda a: a.reshape(1, -1).astype(F32)
    h = x.reshape(batch * seq, d)

    h = _layer0(h, row(mix_norm_g[0]), a_w_in[0].astype(BF16), row(a_ln_g[0]), row(a_ln_b[0]),
                a_w_s[0], a_b_s[0].T, a_w_out[0].astype(BF16), row(mlp_norm_g[0]),
                w_up[0].astype(BF16), w_down[0].astype(BF16))

    qkv = _qkv(h, row(mix_norm_g[1]), b_w_qkv[0].astype(BF16))
    tables = _bias_tables(rel_bias)
    os_, lses = [], []
    for g, (_, dil) in enumerate(WINDOW_DILATIONS):
        rows = min(2048, seq // dil)
        o, lse = _attention_group(qkv[g], qkv[N_DIL_GROUPS + g], qkv[2 * N_DIL_GROUPS + g],
                                  tables[g], batch, seq, dil, rows)
        os_.append(o)
        lses.append(lse)

    out = _layer1_tail(h, os_, lses, b_w_out[0].astype(BF16), row(mlp_norm_g[1]),
                       w_up[1].astype(BF16), w_down[1].astype(BF16), row(final_norm_g))
    return out.reshape(batch, seq, d)
```

```python
import math

import jax
import jax.numpy as jnp
from jax import lax
from jax.experimental import pallas as pl
from jax.experimental.pallas import tpu as pltpu

D_MODEL = 1024
CHUNK = 128
GATE_WIDTH = D_MODEL
GATE_GROUPS = 8
WINDOW_DILATIONS = ((128, 1), (512, 4), (2048, 16))
N_DIL_GROUPS = len(WINDOW_DILATIONS)
ATT_HEADS = 8
HEAD_DIM = 64
ATT_WIDTH = ATT_HEADS * HEAD_DIM
N_BUCKETS = 32
MAX_EXACT = N_BUCKETS // 2
REL_MAX_DISTANCE = max(w for w, _ in WINDOW_DILATIONS)
D_FF = 4 * D_MODEL
EPS = 1e-6
NEG_INF = -1e30
LOG2E = math.log2(math.e)

BLK = 128
LANES = 128
HEADS_PER_VREG = LANES // HEAD_DIM
HEAD_PAIRS = ATT_HEADS // HEADS_PER_VREG
TOKEN_TILE = 1024
SUB_TILE = 512
FF_CHUNK = 1024
ATTN_BLOCKS_PER_STEP = 16
VMEM_LIMIT = 56 * 1024 * 1024

F32 = jnp.float32
BF16 = jnp.bfloat16


def _dot(a, b):
    return jnp.dot(a, b, preferred_element_type=F32)


def _dot_nt(a, b):
    return lax.dot_general(a, b, (((1,), (1,)), ((), ())), preferred_element_type=F32)


def _rms_norm(x, g):
    return x * lax.rsqrt(jnp.mean(x * x, axis=-1, keepdims=True) + EPS) * g


def _gelu_exact(x):
    return 0.5 * x * (1.0 + lax.erf(x * math.sqrt(0.5)))


def _mlp_residual(h, g, wup_ref, wdn_ref):
    xn = _rms_norm(h, g).astype(BF16)
    acc = h
    for c in range(D_FF // FF_CHUNK):
        cols = slice(c * FF_CHUNK, (c + 1) * FF_CHUNK)
        up = _dot(xn, wup_ref[:, cols])
        act = jnp.square(jnp.maximum(up, 0.0)).astype(BF16)
        acc = acc + _dot(act, wdn_ref[cols, :])
    return acc


def _whole(shape):
    return pl.BlockSpec(shape, lambda *_: (0,) * len(shape))


def _layer0_kernel(h_ref, mixg_ref, win_ref, lng_ref, lnb_ref, ws_ref, bst_ref, wout_ref,
                   mlpg_ref, wup_ref, wdn_ref, out_ref):
    nc = SUB_TILE // CHUNK
    t_idx = lax.broadcasted_iota(jnp.int32, (CHUNK, CHUNK), 0)
    s_idx = lax.broadcasted_iota(jnp.int32, (CHUNK, CHUNK), 1)
    causal = s_idx <= t_idx
    for sub in range(h_ref.shape[0] // SUB_TILE):
        rows = slice(sub * SUB_TILE, (sub + 1) * SUB_TILE)
        h = h_ref[rows, :]
        xn = _rms_norm(h, mixg_ref[...]).astype(BF16)
        uv = _gelu_exact(_dot(xn, win_ref[...]))
        u = uv[:, :GATE_WIDTH]
        v = uv[:, GATE_WIDTH:]
        mu = jnp.mean(v, axis=-1, keepdims=True)
        vc = v - mu
        vn = vc * lax.rsqrt(jnp.mean(vc * vc, axis=-1, keepdims=True) + EPS)
        vb = (vn * lng_ref[...] + lnb_ref[...]).astype(BF16)
        mixed = []
        for g in range(GATE_GROUPS):
            wg = jnp.where(causal, ws_ref[g], 0.0).astype(BF16)
            lanes = slice(g * CHUNK, (g + 1) * CHUNK)
            vg = jnp.concatenate([vb[c * CHUNK:(c + 1) * CHUNK, lanes] for c in range(nc)], axis=1)
            mixed.append(_dot(wg, vg) + bst_ref[:, g:g + 1])
        gate = jnp.concatenate(
            [jnp.concatenate([mixed[g][:, c * CHUNK:(c + 1) * CHUNK] for g in range(GATE_GROUPS)], axis=1)
             for c in range(nc)], axis=0)
        h1 = h + _dot((u * gate).astype(BF16), wout_ref[...])
        out_ref[rows, :] = _mlp_residual(h1, mlpg_ref[...], wup_ref, wdn_ref)


def _layer0(h, mix_g, w_in, ln_g, ln_b, w_s, b_st, w_out, mlp_g, w_up, w_down):
    t = h.shape[0]
    tile = pl.BlockSpec((TOKEN_TILE, D_MODEL), lambda i: (i, 0))
    return pl.pallas_call(
        _layer0_kernel,
        grid=(t // TOKEN_TILE,),
        in_specs=[tile, _whole((1, D_MODEL)), _whole(w_in.shape), _whole((1, GATE_WIDTH)),
                  _whole((1, GATE_WIDTH)), _whole(w_s.shape), _whole(b_st.shape), _whole(w_out.shape),
                  _whole((1, D_MODEL)), _whole(w_up.shape), _whole(w_down.shape)],
        out_specs=tile,
        out_shape=jax.ShapeDtypeStruct((t, D_MODEL), F32),
        compiler_params=pltpu.CompilerParams(dimension_semantics=("arbitrary",),
                                             vmem_limit_bytes=VMEM_LIMIT),
        name="layer0_gating_mlp",
    )(h, mix_g, w_in, ln_g, ln_b, w_s, b_st, w_out, mlp_g, w_up, w_down)


def _qkv_kernel(h_ref, g_ref, w_ref, *refs):
    out_refs, xs_ref = refs[:-1], refs[-1]
    n_col = D_MODEL // LANES
    for sub in range(h_ref.shape[0] // SUB_TILE):
        xn = _rms_norm(h_ref[sub * SUB_TILE:(sub + 1) * SUB_TILE, :], g_ref[...])
        for c in range(n_col):
            xs_ref[sub, c] = xn[:, c * LANES:(c + 1) * LANES]
        for grp, (_, dil) in enumerate(WINDOW_DILATIONS):
            per = SUB_TILE // dil
            if dil == 1:
                x = xn
            else:
                x = jnp.concatenate(
                    [jnp.concatenate([xs_ref[sub, c, pl.ds(r, per, stride=dil), :] for c in range(n_col)],
                                     axis=1) for r in range(dil)], axis=0)
            x = x.astype(BF16)
            for part in range(3):
                j = part * N_DIL_GROUPS + grp
                y = _dot(x, w_ref[:, j * ATT_WIDTH:(j + 1) * ATT_WIDTH])
                if part == 0:
                    y = y * (HEAD_DIM ** -0.5 * LOG2E)
                y = y.astype(BF16)
                o_ref = out_refs[j]
                for r in range(dil):
                    o_ref[sub * per:(sub + 1) * per, r * ATT_WIDTH:(r + 1) * ATT_WIDTH] = y[r * per:(r + 1) * per, :]


def _qkv(h, g, w_qkv):
    t = h.shape[0]
    tile = pl.BlockSpec((TOKEN_TILE, D_MODEL), lambda i: (i, 0))
    out_specs, out_shape = [], []
    for _ in range(3):
        for _, dil in WINDOW_DILATIONS:
            out_specs.append(pl.BlockSpec((TOKEN_TILE // dil, dil * ATT_WIDTH), lambda i: (i, 0)))
            out_shape.append(jax.ShapeDtypeStruct((t // dil, dil * ATT_WIDTH), BF16))
    return pl.pallas_call(
        _qkv_kernel,
        grid=(t // TOKEN_TILE,),
        in_specs=[tile, _whole((1, D_MODEL)), _whole(w_qkv.shape)],
        out_specs=out_specs,
        out_shape=out_shape,
        scratch_shapes=[pltpu.VMEM((TOKEN_TILE // SUB_TILE, D_MODEL // LANES, SUB_TILE, LANES), F32)],
        compiler_params=pltpu.CompilerParams(dimension_semantics=("arbitrary",),
                                             vmem_limit_bytes=VMEM_LIMIT),
        name="qkv_proj",
    )(h, g, w_qkv)


def _t5_bucket(distance):
    small = distance < MAX_EXACT
    nf = jnp.maximum(distance, 1).astype(F32)
    large = MAX_EXACT + (jnp.log(nf / MAX_EXACT) / math.log(REL_MAX_DISTANCE / MAX_EXACT)
                         * (N_BUCKETS - MAX_EXACT)).astype(jnp.int32)
    large = jnp.minimum(large, N_BUCKETS - 1)
    return jnp.where(small, distance, large)


def _bias_table_kernel(rb_ref, bucket_ref, out_ref):
    g = pl.program_id(0)
    bucket = bucket_ref[...]
    i_idx = lax.broadcasted_iota(jnp.int32, (BLK, 2 * BLK), 0)
    j_idx = lax.broadcasted_iota(jnp.int32, (BLK, 2 * BLK), 1)
    rel = BLK + i_idx - j_idx
    band = (rel >= 0) & (rel <= BLK)
    for h in range(ATT_HEADS):
        acc = jnp.zeros((BLK, 2 * BLK), F32)
        for b in range(N_BUCKETS):
            acc = jnp.where(bucket == b, rb_ref[b, g * ATT_HEADS + h], acc)
        rows = slice((h % HEADS_PER_VREG) * BLK, (h % HEADS_PER_VREG + 1) * BLK)
        out_ref[h // HEADS_PER_VREG, rows, :] = jnp.where(band, acc * LOG2E, NEG_INF)


def _bias_tables(rel_bias):
    rel = BLK + jnp.arange(BLK)[:, None] - jnp.arange(2 * BLK)[None, :]
    buckets = jnp.stack([_t5_bucket(jnp.clip(rel, 0, BLK) * dil) for _, dil in WINDOW_DILATIONS])
    shape = (N_DIL_GROUPS, HEAD_PAIRS, HEADS_PER_VREG * BLK, 2 * BLK)
    return pl.pallas_call(
        _bias_table_kernel,
        grid=(N_DIL_GROUPS,),
        in_specs=[pl.BlockSpec(memory_space=pltpu.SMEM),
                  pl.BlockSpec((None, BLK, 2 * BLK), lambda g: (g, 0, 0))],
        out_specs=pl.BlockSpec((None,) + shape[1:], lambda g: (g, 0, 0, 0)),
        out_shape=jax.ShapeDtypeStruct(shape, F32),
        name="rel_bias_tables",
    )(rel_bias.astype(F32), buckets.astype(jnp.int32))


def _lse_lane(head):
    return (head % HEADS_PER_VREG) * HEAD_DIM + HEADS_PER_VREG * (head // HEADS_PER_VREG)


def _attn_kernel(q_ref, k_ref, v_ref, kp_ref, vp_ref, tb_ref, o_ref, lse_ref):
    rows = q_ref.shape[0]

    def unit(q, k2, v2, penalty):
        lane = lax.broadcasted_iota(jnp.int32, (BLK, LANES), 1)
        first = lane < HEAD_DIM
        ones = jnp.ones((2 * BLK, LANES), BF16)
        outs = []
        lse = jnp.zeros((BLK, LANES), F32)
        for pair in range(HEAD_PAIRS):
            sl = slice(pair * LANES, (pair + 1) * LANES)
            qp, kp, vp = q[:, sl], k2[:, sl], v2[:, sl]
            zero = jnp.zeros_like(qp)
            qq = jnp.concatenate([jnp.where(first, qp, zero), jnp.where(first, zero, qp)], axis=0)
            logits = _dot_nt(qq, kp) + tb_ref[pair]
            if penalty is not None:
                logits = logits + penalty
            m = jnp.max(logits, axis=-1, keepdims=True)
            p = jnp.exp2(logits - m).astype(BF16)
            res = _dot(p, jnp.concatenate([vp, ones], axis=1))
            num = jnp.where(first, res[:BLK, :LANES], res[BLK:, :LANES])
            den = jnp.where(first, res[:BLK, LANES:], res[BLK:, LANES:])
            outs.append((num / den).astype(BF16))
            lse_pair = jnp.where(first, m[:BLK], m[BLK:]) + jnp.log2(den)
            keep = (lane == _lse_lane(pair * HEADS_PER_VREG)) | (lane == _lse_lane(pair * HEADS_PER_VREG + 1))
            lse = jnp.where(keep, lse_pair, lse)
        return jnp.concatenate(outs, axis=1), lse

    j_idx = lax.broadcasted_iota(jnp.int32, (2 * BLK, 2 * BLK), 1)
    no_prev = pl.program_id(2) == 0
    penalty = jnp.where((j_idx < BLK) & no_prev, NEG_INF, 0.0)
    for s in range(q_ref.shape[1] // ATT_WIDTH):
        cols = slice(s * ATT_WIDTH, (s + 1) * ATT_WIDTH)
        lse_cols = slice(s * LANES, (s + 1) * LANES)
        for u in range(rows // BLK):
            cur_rows = slice(u * BLK, (u + 1) * BLK)
            if u == 0:
                k2 = jnp.concatenate([kp_ref[:, cols], k_ref[cur_rows, cols]], axis=0)
                v2 = jnp.concatenate([vp_ref[:, cols], v_ref[cur_rows, cols]], axis=0)
                o, lse = unit(q_ref[cur_rows, cols], k2, v2, penalty)
            else:
                both = slice((u - 1) * BLK, (u + 1) * BLK)
                o, lse = unit(q_ref[cur_rows, cols], k_ref[both, cols], v_ref[both, cols], None)
            o_ref[cur_rows, cols] = o
            lse_ref[cur_rows, lse_cols] = lse


def _attention_group(q, k, v, table, batch, seq, dil):
    sub = seq // dil
    w = ATT_WIDTH
    rows = min(ATTN_BLOCKS_PER_STEP * BLK, sub)
    n_seq = ATTN_BLOCKS_PER_STEP * BLK // rows
    assert sub % rows == 0 and dil % n_seq == 0
    q, k, v = (a.reshape(batch, sub, dil * w) for a in (q, k, v))
    blocks_per_step = rows // BLK
    cur = pl.BlockSpec((None, rows, n_seq * w), lambda b, r, i: (b, i, r))
    prev = pl.BlockSpec((None, BLK, n_seq * w),
                        lambda b, r, i: (b, jnp.maximum(i * blocks_per_step - 1, 0), r))
    o, lse = pl.pallas_call(
        _attn_kernel,
        grid=(batch, dil // n_seq, sub // rows),
        in_specs=[cur, cur, cur, prev, prev, _whole(table.shape)],
        out_specs=[cur, pl.BlockSpec((None, rows, n_seq * LANES), lambda b, r, i: (b, i, r))],
        out_shape=[jax.ShapeDtypeStruct((batch, sub, dil * w), BF16),
                   jax.ShapeDtypeStruct((batch, sub, dil * LANES), F32)],
        compiler_params=pltpu.CompilerParams(
            dimension_semantics=("arbitrary", "arbitrary", "arbitrary"),
            vmem_limit_bytes=VMEM_LIMIT),
        name=f"dilated_attention_d{dil}",
    )(q, k, v, k, v, table)
    return o.reshape(batch * sub, dil * w), lse.reshape(batch * sub, dil * LANES)


def _natural_order(ref, scr_ref, sub, dil, width):
    per = SUB_TILE // dil
    rows = slice(sub * per, (sub + 1) * per)
    if dil == 1:
        return ref[rows, :].astype(F32)
    n_col = width // LANES
    for r in range(dil):
        for c in range(n_col):
            lanes = slice(r * width + c * LANES, r * width + (c + 1) * LANES)
            scr_ref[c, pl.ds(r, per, stride=dil), :] = ref[rows, lanes].astype(F32)
    return jnp.concatenate([scr_ref[c] for c in range(n_col)], axis=1)


def _layer1_tail_kernel(h_ref, o0_ref, o1_ref, o2_ref, l0_ref, l1_ref, l2_ref, wout_ref,
                        mlpg_ref, wup_ref, wdn_ref, fing_ref, out_ref, os_ref, ls_ref):
    dils = [d for _, d in WINDOW_DILATIONS]
    row = lax.broadcasted_iota(jnp.int32, (LANES, ATT_WIDTH), 0)
    head = lax.broadcasted_iota(jnp.int32, (LANES, ATT_WIDTH), 1) // HEAD_DIM
    src = (head % HEADS_PER_VREG) * HEAD_DIM + HEADS_PER_VREG * (head // HEADS_PER_VREG)
    expand = jnp.where(row == src, 1.0, 0.0).astype(BF16)
    for sub in range(h_ref.shape[0] // SUB_TILE):
        rows = slice(sub * SUB_TILE, (sub + 1) * SUB_TILE)
        lses = [_natural_order(l_ref, ls_ref.at[sub, g], sub, dils[g], LANES)
                for g, l_ref in enumerate((l0_ref, l1_ref, l2_ref))]
        outs = [_natural_order(o_ref, os_ref.at[sub, g], sub, dils[g], ATT_WIDTH)
                for g, o_ref in enumerate((o0_ref, o1_ref, o2_ref))]
        m_all = jnp.maximum(jnp.maximum(lses[0], lses[1]), lses[2])
        num = jnp.zeros(outs[0].shape, F32)
        den = jnp.zeros(outs[0].shape, F32)
        for l, og in zip(lses, outs):
            wide = _dot(jnp.exp2(l - m_all).astype(BF16), expand)
            num = num + wide * og
            den = den + wide
        h1 = h_ref[rows, :] + _dot((num / den).astype(BF16), wout_ref[...])
        h2 = _mlp_residual(h1, mlpg_ref[...], wup_ref, wdn_ref)
        out_ref[rows, :] = _rms_norm(h2, fing_ref[...])


def _layer1_tail(h, os_, lses, w_out, mlp_g, w_up, w_down, fin_g):
    t = h.shape[0]
    tile = pl.BlockSpec((TOKEN_TILE, D_MODEL), lambda i: (i, 0))
    otiles = [pl.BlockSpec((TOKEN_TILE // d, d * ATT_WIDTH), lambda i: (i, 0)) for _, d in WINDOW_DILATIONS]
    ltiles = [pl.BlockSpec((TOKEN_TILE // d, d * LANES), lambda i: (i, 0)) for _, d in WINDOW_DILATIONS]
    return pl.pallas_call(
        _layer1_tail_kernel,
        grid=(t // TOKEN_TILE,),
        in_specs=[tile, *otiles, *ltiles, _whole(w_out.shape),
                  _whole((1, D_MODEL)), _whole(w_up.shape), _whole(w_down.shape),
                  _whole((1, D_MODEL))],
        out_specs=tile,
        out_shape=jax.ShapeDtypeStruct((t, D_MODEL), F32),
        scratch_shapes=[
            pltpu.VMEM((TOKEN_TILE // SUB_TILE, N_DIL_GROUPS, ATT_WIDTH // LANES, SUB_TILE, LANES), F32),
            pltpu.VMEM((TOKEN_TILE // SUB_TILE, N_DIL_GROUPS, 1, SUB_TILE, LANES), F32)],
        compiler_params=pltpu.CompilerParams(dimension_semantics=("arbitrary",),
                                             vmem_limit_bytes=VMEM_LIMIT),
        name="layer1_merge_proj_mlp",
    )(h, *os_, *lses, w_out, mlp_g, w_up, w_down, fin_g)


def kernel(x, mix_norm_g, mlp_norm_g, final_norm_g, a_w_in, a_ln_g, a_ln_b, a_w_s, a_b_s, a_w_out,
           b_w_qkv, b_w_out, rel_bias, w_up, w_down):
    batch, seq, d = x.shape
    max_dil = WINDOW_DILATIONS[-1][1]
    assert d == D_MODEL and seq % (BLK * max_dil) == 0
    assert seq % TOKEN_TILE == 0 and TOKEN_TILE % SUB_TILE == 0
    assert SUB_TILE % CHUNK == 0 and SUB_TILE % (16 * max_dil) == 0
    row = lambda a: a.reshape(1, -1).astype(F32)
    h = x.reshape(batch * seq, d)

    h = _layer0(h, row(mix_norm_g[0]), a_w_in[0].astype(BF16), row(a_ln_g[0]), row(a_ln_b[0]),
                a_w_s[0], a_b_s[0].T, a_w_out[0].astype(BF16), row(mlp_norm_g[0]),
                w_up[0].astype(BF16), w_down[0].astype(BF16))

    qkv = _qkv(h, row(mix_norm_g[1]), b_w_qkv[0].astype(BF16))
    tables = _bias_tables(rel_bias)
    os_, lses = [], []
    for g, (_, dil) in enumerate(WINDOW_DILATIONS):
        o, lse = _attention_group(qkv[g], qkv[N_DIL_GROUPS + g], qkv[2 * N_DIL_GROUPS + g],
                                  tables[g], batch, seq, dil)
        os_.append(o)
        lses.append(lse)

    out = _layer1_tail(h, os_, lses, b_w_out[0].astype(BF16), row(mlp_norm_g[1]),
                       w_up[1].astype(BF16), w_down[1].astype(BF16), row(final_norm_g))
    return out.reshape(batch, seq, d)
```

```python
import functools
import math

import jax
import jax.numpy as jnp
from jax import lax
from jax.experimental import pallas as pl
from jax.experimental.pallas import tpu as pltpu

D_MODEL = 1024
CHUNK = 128
GATE_WIDTH = D_MODEL
GATE_GROUPS = 8
WINDOW_DILATIONS = ((128, 1), (512, 4), (2048, 16))
N_DIL_GROUPS = len(WINDOW_DILATIONS)
ATT_HEADS = 8
HEAD_DIM = 64
ATT_WIDTH = ATT_HEADS * HEAD_DIM
N_BUCKETS = 32
MAX_EXACT = N_BUCKETS // 2
REL_MAX_DISTANCE = max(w for w, _ in WINDOW_DILATIONS)
D_FF = 4 * D_MODEL
EPS = 1e-6
NEG_INF = -1e30
LOG2E = math.log2(math.e)

BLK = 128
LANES = 128
HEADS_PER_VREG = LANES // HEAD_DIM
HEAD_PAIRS = ATT_HEADS // HEADS_PER_VREG
TOKEN_TILE = 1024
SUB_TILE = 512
FF_CHUNK = 1024
ATTN_BLOCKS_PER_STEP = 16
VMEM_LIMIT = 56 * 1024 * 1024

F32 = jnp.float32
BF16 = jnp.bfloat16


def _dot(a, b):
    return jnp.dot(a, b, preferred_element_type=F32)


def _dot_nt(a, b):
    return lax.dot_general(a, b, (((1,), (1,)), ((), ())), preferred_element_type=F32)


def _rms_norm(x, g):
    return x * lax.rsqrt(jnp.mean(x * x, axis=-1, keepdims=True) + EPS) * g


def _gelu_exact(x):
    return 0.5 * x * (1.0 + lax.erf(x * math.sqrt(0.5)))


def _mlp_residual(h, g, wup_ref, wdn_ref):
    xn = _rms_norm(h, g).astype(BF16)
    acc = h
    for c in range(D_FF // FF_CHUNK):
        cols = slice(c * FF_CHUNK, (c + 1) * FF_CHUNK)
        up = _dot(xn, wup_ref[:, cols])
        act = jnp.square(jnp.maximum(up, 0.0)).astype(BF16)
        acc = acc + _dot(act, wdn_ref[cols, :])
    return acc


def _whole(shape):
    return pl.BlockSpec(shape, lambda *_: (0,) * len(shape))


def _layer_of(stacked, layer):
    return pl.BlockSpec((None,) + stacked.shape[1:], lambda *_: (layer, 0, 0),
                        pipeline_mode=pl.Buffered(1))


def _layer0_kernel(h_ref, mixg_ref, win_ref, lng_ref, lnb_ref, ws_ref, bst_ref, wout_ref,
                   mlpg_ref, wup_ref, wdn_ref, out_ref):
    nc = SUB_TILE // CHUNK
    t_idx = lax.broadcasted_iota(jnp.int32, (CHUNK, CHUNK), 0)
    s_idx = lax.broadcasted_iota(jnp.int32, (CHUNK, CHUNK), 1)
    causal = s_idx <= t_idx
    for sub in range(h_ref.shape[0] // SUB_TILE):
        rows = slice(sub * SUB_TILE, (sub + 1) * SUB_TILE)
        h = h_ref[rows, :]
        xn = _rms_norm(h, mixg_ref[...]).astype(BF16)
        uv = _gelu_exact(_dot(xn, win_ref[...]))
        u = uv[:, :GATE_WIDTH]
        v = uv[:, GATE_WIDTH:]
        mu = jnp.mean(v, axis=-1, keepdims=True)
        vc = v - mu
        vn = vc * lax.rsqrt(jnp.mean(vc * vc, axis=-1, keepdims=True) + EPS)
        vb = (vn * lng_ref[...] + lnb_ref[...]).astype(BF16)
        mixed = []
        for g in range(GATE_GROUPS):
            wg = jnp.where(causal, ws_ref[g], 0.0).astype(BF16)
            lanes = slice(g * CHUNK, (g + 1) * CHUNK)
            vg = jnp.concatenate([vb[c * CHUNK:(c + 1) * CHUNK, lanes] for c in range(nc)], axis=1)
            mixed.append(_dot(wg, vg) + bst_ref[:, g:g + 1])
        gate = jnp.concatenate(
            [jnp.concatenate([mixed[g][:, c * CHUNK:(c + 1) * CHUNK] for g in range(GATE_GROUPS)], axis=1)
             for c in range(nc)], axis=0)
        h1 = h + _dot((u * gate).astype(BF16), wout_ref[...])
        out_ref[rows, :] = _mlp_residual(h1, mlpg_ref[...], wup_ref, wdn_ref)


def _layer0(h, mix_g, w_in, ln_g, ln_b, w_s, b_st, w_out, mlp_g, w_up, w_down, layer):
    t = h.shape[0]
    tile = pl.BlockSpec((TOKEN_TILE, D_MODEL), lambda i: (i, 0))
    return pl.pallas_call(
        _layer0_kernel,
        grid=(t // TOKEN_TILE,),
        in_specs=[tile, _whole((1, D_MODEL)), _whole(w_in.shape), _whole((1, GATE_WIDTH)),
                  _whole((1, GATE_WIDTH)), _whole(w_s.shape), _whole(b_st.shape), _whole(w_out.shape),
                  _whole((1, D_MODEL)), _layer_of(w_up, layer), _layer_of(w_down, layer)],
        out_specs=tile,
        out_shape=jax.ShapeDtypeStruct((t, D_MODEL), F32),
        compiler_params=pltpu.CompilerParams(dimension_semantics=("arbitrary",),
                                             vmem_limit_bytes=VMEM_LIMIT),
        name="layer0_gating_mlp",
    )(h, mix_g, w_in, ln_g, ln_b, w_s, b_st, w_out, mlp_g, w_up, w_down)


def _qkv_kernel(h_ref, g_ref, w_ref, *refs):
    out_refs, xs_ref = refs[:-1], refs[-1]
    n_col = D_MODEL // LANES
    for sub in range(h_ref.shape[0] // SUB_TILE):
        xn = _rms_norm(h_ref[sub * SUB_TILE:(sub + 1) * SUB_TILE, :], g_ref[...])
        for c in range(n_col):
            xs_ref[sub, c] = xn[:, c * LANES:(c + 1) * LANES]
        for grp, (_, dil) in enumerate(WINDOW_DILATIONS):
            per = SUB_TILE // dil
            if dil == 1:
                x = xn
            else:
                x = jnp.concatenate(
                    [jnp.concatenate([xs_ref[sub, c, pl.ds(r, per, stride=dil), :] for c in range(n_col)],
                                     axis=1) for r in range(dil)], axis=0)
            x = x.astype(BF16)
            for part in range(3):
                j = part * N_DIL_GROUPS + grp
                y = _dot(x, w_ref[:, j * ATT_WIDTH:(j + 1) * ATT_WIDTH])
                if part == 0:
                    y = y * (HEAD_DIM ** -0.5 * LOG2E)
                y = y.astype(BF16)
                o_ref = out_refs[j]
                for r in range(dil):
                    o_ref[sub * per:(sub + 1) * per, r * ATT_WIDTH:(r + 1) * ATT_WIDTH] = y[r * per:(r + 1) * per, :]


def _qkv(h, g, w_qkv):
    t = h.shape[0]
    tile = pl.BlockSpec((TOKEN_TILE, D_MODEL), lambda i: (i, 0))
    out_specs, out_shape = [], []
    for _ in range(3):
        for _, dil in WINDOW_DILATIONS:
            out_specs.append(pl.BlockSpec((TOKEN_TILE // dil, dil * ATT_WIDTH), lambda i: (i, 0)))
            out_shape.append(jax.ShapeDtypeStruct((t // dil, dil * ATT_WIDTH), BF16))
    return pl.pallas_call(
        _qkv_kernel,
        grid=(t // TOKEN_TILE,),
        in_specs=[tile, _whole((1, D_MODEL)), _whole(w_qkv.shape)],
        out_specs=out_specs,
        out_shape=out_shape,
        scratch_shapes=[pltpu.VMEM((TOKEN_TILE // SUB_TILE, D_MODEL // LANES, SUB_TILE, LANES), F32)],
        compiler_params=pltpu.CompilerParams(dimension_semantics=("arbitrary",),
                                             vmem_limit_bytes=VMEM_LIMIT),
        name="qkv_proj",
    )(h, g, w_qkv)


def _t5_bucket(distance):
    small = distance < MAX_EXACT
    nf = jnp.maximum(distance, 1).astype(F32)
    large = MAX_EXACT + (jnp.log(nf / MAX_EXACT) / math.log(REL_MAX_DISTANCE / MAX_EXACT)
                         * (N_BUCKETS - MAX_EXACT)).astype(jnp.int32)
    large = jnp.minimum(large, N_BUCKETS - 1)
    return jnp.where(small, distance, large)


def _bias_table_kernel(rb_ref, bucket_ref, out_ref):
    g = pl.program_id(0)
    bucket = bucket_ref[...]
    i_idx = lax.broadcasted_iota(jnp.int32, (BLK, 2 * BLK), 0)
    j_idx = lax.broadcasted_iota(jnp.int32, (BLK, 2 * BLK), 1)
    rel = BLK + i_idx - j_idx
    band = (rel >= 0) & (rel <= BLK)
    for h in range(ATT_HEADS):
        acc = jnp.zeros((BLK, 2 * BLK), F32)
        for b in range(N_BUCKETS):
            acc = jnp.where(bucket == b, rb_ref[b, g * ATT_HEADS + h], acc)
        rows = slice((h % HEADS_PER_VREG) * BLK, (h % HEADS_PER_VREG + 1) * BLK)
        out_ref[h // HEADS_PER_VREG, rows, :] = jnp.where(band, acc * LOG2E, NEG_INF)


def _bias_tables(rel_bias):
    rel = BLK + jnp.arange(BLK)[:, None] - jnp.arange(2 * BLK)[None, :]
    buckets = jnp.stack([_t5_bucket(jnp.clip(rel, 0, BLK) * dil) for _, dil in WINDOW_DILATIONS])
    shape = (N_DIL_GROUPS, HEAD_PAIRS, HEADS_PER_VREG * BLK, 2 * BLK)
    return pl.pallas_call(
        _bias_table_kernel,
        grid=(N_DIL_GROUPS,),
        in_specs=[pl.BlockSpec(memory_space=pltpu.SMEM),
                  pl.BlockSpec((None, BLK, 2 * BLK), lambda g: (g, 0, 0))],
        out_specs=pl.BlockSpec((None,) + shape[1:], lambda g: (g, 0, 0, 0)),
        out_shape=jax.ShapeDtypeStruct(shape, F32),
        name="rel_bias_tables",
    )(rel_bias.astype(F32), buckets.astype(jnp.int32))


def _lse_lane(head):
    return (head % HEADS_PER_VREG) * HEAD_DIM + HEADS_PER_VREG * (head // HEADS_PER_VREG)


def _attn_kernel(*refs, whole_sequence):
    if whole_sequence:
        q_ref, k_ref, v_ref, tb_ref, o_ref, lse_ref = refs
    else:
        q_ref, k_ref, v_ref, kp_ref, vp_ref, tb_ref, o_ref, lse_ref = refs
    rows = q_ref.shape[0]

    def unit(q, keys, vals, penalty):
        lane = lax.broadcasted_iota(jnp.int32, (BLK, LANES), 1)
        first = lane < HEAD_DIM
        ones = jnp.ones((2 * BLK, LANES), BF16)
        outs = []
        lse = jnp.zeros((BLK, LANES), F32)
        for pair in range(HEAD_PAIRS):
            sl = slice(pair * LANES, (pair + 1) * LANES)
            qp, kp, vp = q[:, sl], keys[:, sl], vals[:, sl]
            zero = jnp.zeros_like(qp)
            qq = jnp.concatenate([jnp.where(first, qp, zero), jnp.where(first, zero, qp)], axis=0)
            logits = _dot_nt(qq, kp) + tb_ref[pair]
            if penalty is not None:
                logits = logits + penalty
            m = jnp.max(logits, axis=-1, keepdims=True)
            p = jnp.exp2(logits - m).astype(BF16)
            res = _dot(p, jnp.concatenate([vp, ones], axis=1))
            num = jnp.where(first, res[:BLK, :LANES], res[BLK:, :LANES])
            den = jnp.where(first, res[:BLK, LANES:], res[BLK:, LANES:])
            outs.append((num / den).astype(BF16))
            lse_pair = jnp.where(first, m[:BLK], m[BLK:]) + jnp.log2(den)
            keep = (lane == _lse_lane(pair * HEADS_PER_VREG)) | (lane == _lse_lane(pair * HEADS_PER_VREG + 1))
            lse = jnp.where(keep, lse_pair, lse)
        return jnp.concatenate(outs, axis=1), lse

    at_start = True if whole_sequence else pl.program_id(2) == 0
    j_idx = lax.broadcasted_iota(jnp.int32, (2 * BLK, 2 * BLK), 1)
    penalty = jnp.where((j_idx < BLK) & at_start, NEG_INF, 0.0)
    for s in range(q_ref.shape[1] // ATT_WIDTH):
        cols = slice(s * ATT_WIDTH, (s + 1) * ATT_WIDTH)
        lse_cols = slice(s * LANES, (s + 1) * LANES)
        for u in range(rows // BLK):
            cur_rows = slice(u * BLK, (u + 1) * BLK)
            if u > 0:
                both = slice((u - 1) * BLK, (u + 1) * BLK)
                o, lse = unit(q_ref[cur_rows, cols], k_ref[both, cols], v_ref[both, cols], None)
            else:
                k_before = k_ref[cur_rows, cols] if whole_sequence else kp_ref[:, cols]
                v_before = v_ref[cur_rows, cols] if whole_sequence else vp_ref[:, cols]
                keys = jnp.concatenate([k_before, k_ref[cur_rows, cols]], axis=0)
                vals = jnp.concatenate([v_before, v_ref[cur_rows, cols]], axis=0)
                o, lse = unit(q_ref[cur_rows, cols], keys, vals, penalty)
            o_ref[cur_rows, cols] = o
            lse_ref[cur_rows, lse_cols] = lse


def _attention_group(q, k, v, table, batch, seq, dil):
    sub = seq // dil
    w = ATT_WIDTH
    rows = min(ATTN_BLOCKS_PER_STEP * BLK, sub)
    n_seq = ATTN_BLOCKS_PER_STEP * BLK // rows
    assert sub % rows == 0 and dil % n_seq == 0
    whole_sequence = rows == sub
    q, k, v = (a.reshape(batch, sub, dil * w) for a in (q, k, v))
    blocks_per_step = rows // BLK
    cur = pl.BlockSpec((None, rows, n_seq * w), lambda b, r, i: (b, i, r))
    prev = pl.BlockSpec((None, BLK, n_seq * w),
                        lambda b, r, i: (b, jnp.maximum(i * blocks_per_step - 1, 0), r))
    operands = (q, k, v) if whole_sequence else (q, k, v, k, v)
    in_specs = [cur, cur, cur] if whole_sequence else [cur, cur, cur, prev, prev]
    o, lse = pl.pallas_call(
        functools.partial(_attn_kernel, whole_sequence=whole_sequence),
        grid=(batch, dil // n_seq, sub // rows),
        in_specs=in_specs + [_whole(table.shape)],
        out_specs=[cur, pl.BlockSpec((None, rows, n_seq * LANES), lambda b, r, i: (b, i, r))],
        out_shape=[jax.ShapeDtypeStruct((batch, sub, dil * w), BF16),
                   jax.ShapeDtypeStruct((batch, sub, dil * LANES), F32)],
        compiler_params=pltpu.CompilerParams(
            dimension_semantics=("arbitrary", "arbitrary", "arbitrary"),
            vmem_limit_bytes=VMEM_LIMIT),
        name=f"dilated_attention_d{dil}",
    )(*operands, table)
    return o.reshape(batch * sub, dil * w), lse.reshape(batch * sub, dil * LANES)


def _natural_order(ref, scr_ref, sub, dil, width):
    per = SUB_TILE // dil
    rows = slice(sub * per, (sub + 1) * per)
    if dil == 1:
        return ref[rows, :].astype(F32)
    n_col = width // LANES
    for r in range(dil):
        for c in range(n_col):
            lanes = slice(r * width + c * LANES, r * width + (c + 1) * LANES)
            scr_ref[c, pl.ds(r, per, stride=dil), :] = ref[rows, lanes].astype(F32)
    return jnp.concatenate([scr_ref[c] for c in range(n_col)], axis=1)


def _layer1_tail_kernel(h_ref, o0_ref, o1_ref, o2_ref, l0_ref, l1_ref, l2_ref, wout_ref,
                        mlpg_ref, wup_ref, wdn_ref, fing_ref, out_ref, os_ref, ls_ref):
    dils = [d for _, d in WINDOW_DILATIONS]
    row = lax.broadcasted_iota(jnp.int32, (LANES, ATT_WIDTH), 0)
    head = lax.broadcasted_iota(jnp.int32, (LANES, ATT_WIDTH), 1) // HEAD_DIM
    src = (head % HEADS_PER_VREG) * HEAD_DIM + HEADS_PER_VREG * (head // HEADS_PER_VREG)
    expand = jnp.where(row == src, 1.0, 0.0).astype(BF16)
    for sub in range(h_ref.shape[0] // SUB_TILE):
        rows = slice(sub * SUB_TILE, (sub + 1) * SUB_TILE)
        lses = [_natural_order(l_ref, ls_ref.at[sub, g], sub, dils[g], LANES)
                for g, l_ref in enumerate((l0_ref, l1_ref, l2_ref))]
        outs = [_natural_order(o_ref, os_ref.at[sub, g], sub, dils[g], ATT_WIDTH)
                for g, o_ref in enumerate((o0_ref, o1_ref, o2_ref))]
        m_all = jnp.maximum(jnp.maximum(lses[0], lses[1]), lses[2])
        num = jnp.zeros(outs[0].shape, F32)
        den = jnp.zeros(outs[0].shape, F32)
        for l, og in zip(lses, outs):
            wide = _dot(jnp.exp2(l - m_all).astype(BF16), expand)
            num = num + wide * og
            den = den + wide
        h1 = h_ref[rows, :] + _dot((num / den).astype(BF16), wout_ref[...])
        h2 = _mlp_residual(h1, mlpg_ref[...], wup_ref, wdn_ref)
        out_ref[rows, :] = _rms_norm(h2, fing_ref[...])


def _layer1_tail(h, os_, lses, w_out, mlp_g, w_up, w_down, fin_g, layer):
    t = h.shape[0]
    tile = pl.BlockSpec((TOKEN_TILE, D_MODEL), lambda i: (i, 0))
    otiles = [pl.BlockSpec((TOKEN_TILE // d, d * ATT_WIDTH), lambda i: (i, 0)) for _, d in WINDOW_DILATIONS]
    ltiles = [pl.BlockSpec((TOKEN_TILE // d, d * LANES), lambda i: (i, 0)) for _, d in WINDOW_DILATIONS]
    return pl.pallas_call(
        _layer1_tail_kernel,
        grid=(t // TOKEN_TILE,),
        in_specs=[tile, *otiles, *ltiles, _whole(w_out.shape),
                  _whole((1, D_MODEL)), _layer_of(w_up, layer), _layer_of(w_down, layer),
                  _whole((1, D_MODEL))],
        out_specs=tile,
        out_shape=jax.ShapeDtypeStruct((t, D_MODEL), F32),
        scratch_shapes=[
            pltpu.VMEM((TOKEN_TILE // SUB_TILE, N_DIL_GROUPS, ATT_WIDTH // LANES, SUB_TILE, LANES), F32),
            pltpu.VMEM((TOKEN_TILE // SUB_TILE, N_DIL_GROUPS, 1, SUB_TILE, LANES), F32)],
        compiler_params=pltpu.CompilerParams(dimension_semantics=("arbitrary",),
                                             vmem_limit_bytes=VMEM_LIMIT),
        name="layer1_merge_proj_mlp",
    )(h, *os_, *lses, w_out, mlp_g, w_up, w_down, fin_g)


def kernel(x, mix_norm_g, mlp_norm_g, final_norm_g, a_w_in, a_ln_g, a_ln_b, a_w_s, a_b_s, a_w_out,
           b_w_qkv, b_w_out, rel_bias, w_up, w_down):
    batch, seq, d = x.shape
    max_dil = WINDOW_DILATIONS[-1][1]
    assert d == D_MODEL and seq % (BLK * max_dil) == 0
    assert seq % TOKEN_TILE == 0 and TOKEN_TILE % SUB_TILE == 0
    assert SUB_TILE % CHUNK == 0 and SUB_TILE % (16 * max_dil) == 0
    row = lambda a: a.reshape(1, -1).astype(F32)
    h = x.reshape(batch * seq, d)

    w_up, w_down = w_up.astype(BF16), w_down.astype(BF16)
    h = _layer0(h, row(mix_norm_g[0]), a_w_in[0].astype(BF16), row(a_ln_g[0]), row(a_ln_b[0]),
                a_w_s[0], a_b_s[0].T, a_w_out[0].astype(BF16), row(mlp_norm_g[0]), w_up, w_down, 0)

    qkv = _qkv(h, row(mix_norm_g[1]), b_w_qkv[0].astype(BF16))
    tables = _bias_tables(rel_bias)
    os_, lses = [], []
    for g, (_, dil) in enumerate(WINDOW_DILATIONS):
        o, lse = _attention_group(qkv[g], qkv[N_DIL_GROUPS + g], qkv[2 * N_DIL_GROUPS + g],
                                  tables[g], batch, seq, dil)
        os_.append(o)
        lses.append(lse)

    out = _layer1_tail(h, os_, lses, b_w_out[0].astype(BF16), row(mlp_norm_g[1]), w_up, w_down,
                       row(final_norm_g), 1)
    return out.reshape(batch, seq, d)
```

```python
import functools
import math

import jax
import jax.numpy as jnp
from jax import lax
from jax.experimental import pallas as pl
from jax.experimental.pallas import tpu as pltpu

D_MODEL = 1024
CHUNK = 128
GATE_WIDTH = D_MODEL
GATE_GROUPS = 8
WINDOW_DILATIONS = ((128, 1), (512, 4), (2048, 16))
N_DIL_GROUPS = len(WINDOW_DILATIONS)
ATT_HEADS = 8
HEAD_DIM = 64
ATT_WIDTH = ATT_HEADS * HEAD_DIM
N_BUCKETS = 32
MAX_EXACT = N_BUCKETS // 2
REL_MAX_DISTANCE = max(w for w, _ in WINDOW_DILATIONS)
D_FF = 4 * D_MODEL
EPS = 1e-6
NEG_INF = -1e30
LOG2E = math.log2(math.e)

BLK = 128
LANES = 128
HEADS_PER_VREG = LANES // HEAD_DIM
HEAD_PAIRS = ATT_HEADS // HEADS_PER_VREG
TOKEN_TILE = 1024
SUB_TILE = 512
CHAIN_LAG = 2
FF_CHUNK = 1024
ATTN_BLOCKS_PER_STEP = 16
VMEM_LIMIT = 63 * 1024 * 1024

F32 = jnp.float32
BF16 = jnp.bfloat16


def _dot(a, b):
    return jnp.dot(a, b, preferred_element_type=F32)


def _dot_nt(a, b):
    return lax.dot_general(a, b, (((1,), (1,)), ((), ())), preferred_element_type=F32)


def _rms_norm(x, g):
    return x * lax.rsqrt(jnp.mean(x * x, axis=-1, keepdims=True) + EPS) * g


def _gelu_exact(x):
    return 0.5 * x * (1.0 + lax.erf(x * math.sqrt(0.5)))


def _interleave(chains, lag):
    chains = list(chains)
    alive = [True] * len(chains)
    t = 0
    while any(alive):
        for i, chain in enumerate(chains):
            if alive[i] and t >= i * lag:
                try:
                    next(chain)
                except StopIteration:
                    alive[i] = False
        t += 1


def _mlp_residual(h, g, wup_ref, wdn_ref):
    xn = _rms_norm(h, g).astype(BF16)
    acc = h
    for c in range(D_FF // FF_CHUNK):
        cols = slice(c * FF_CHUNK, (c + 1) * FF_CHUNK)
        up = _dot(xn, wup_ref[:, cols])
        yield
        act = jnp.square(jnp.maximum(up, 0.0)).astype(BF16)
        acc = acc + _dot(act, wdn_ref[cols, :])
        yield
    return acc


def _whole(shape):
    return pl.BlockSpec(shape, lambda *_: (0,) * len(shape))


def _layer_of(stacked, layer):
    return pl.BlockSpec((None,) + stacked.shape[1:], lambda *_: (layer, 0, 0),
                        pipeline_mode=pl.Buffered(1))


def _layer0_chain(rows, causal, h_ref, mixg_ref, win_ref, lng_ref, lnb_ref, ws_ref, bst_ref,
                  wout_ref, mlpg_ref, wup_ref, wdn_ref, out_ref):
    nc = SUB_TILE // CHUNK
    h = h_ref[rows, :]
    xn = _rms_norm(h, mixg_ref[...]).astype(BF16)
    v = _gelu_exact(_dot(xn, win_ref[:, GATE_WIDTH:]))
    yield
    u = _gelu_exact(_dot(xn, win_ref[:, :GATE_WIDTH]))
    yield
    mu = jnp.mean(v, axis=-1, keepdims=True)
    vc = v - mu
    vn = vc * lax.rsqrt(jnp.mean(vc * vc, axis=-1, keepdims=True) + EPS)
    vb = (vn * lng_ref[...] + lnb_ref[...]).astype(BF16)
    mixed = []
    for g in range(GATE_GROUPS):
        wg = jnp.where(causal, ws_ref[g], 0.0).astype(BF16)
        lanes = slice(g * CHUNK, (g + 1) * CHUNK)
        vg = jnp.concatenate([vb[c * CHUNK:(c + 1) * CHUNK, lanes] for c in range(nc)], axis=1)
        mixed.append(_dot(wg, vg) + bst_ref[:, g:g + 1])
    gate = jnp.concatenate(
        [jnp.concatenate([mixed[g][:, c * CHUNK:(c + 1) * CHUNK] for g in range(GATE_GROUPS)], axis=1)
         for c in range(nc)], axis=0)
    yield
    h1 = h + _dot((u * gate).astype(BF16), wout_ref[...])
    yield
    out_ref[rows, :] = yield from _mlp_residual(h1, mlpg_ref[...], wup_ref, wdn_ref)


def _layer0_kernel(h_ref, *refs):
    t_idx = lax.broadcasted_iota(jnp.int32, (CHUNK, CHUNK), 0)
    s_idx = lax.broadcasted_iota(jnp.int32, (CHUNK, CHUNK), 1)
    causal = s_idx <= t_idx
    _interleave((_layer0_chain(slice(sub * SUB_TILE, (sub + 1) * SUB_TILE), causal, h_ref, *refs)
                 for sub in range(h_ref.shape[0] // SUB_TILE)), CHAIN_LAG)


def _layer0(h, mix_g, w_in, ln_g, ln_b, w_s, b_st, w_out, mlp_g, w_up, w_down, layer):
    t = h.shape[0]
    tile = pl.BlockSpec((TOKEN_TILE, D_MODEL), lambda i: (i, 0))
    return pl.pallas_call(
        _layer0_kernel,
        grid=(t // TOKEN_TILE,),
        in_specs=[tile, _whole((1, D_MODEL)), _whole(w_in.shape), _whole((1, GATE_WIDTH)),
                  _whole((1, GATE_WIDTH)), _whole(w_s.shape), _whole(b_st.shape), _whole(w_out.shape),
                  _whole((1, D_MODEL)), _layer_of(w_up, layer), _layer_of(w_down, layer)],
        out_specs=tile,
        out_shape=jax.ShapeDtypeStruct((t, D_MODEL), F32),
        compiler_params=pltpu.CompilerParams(dimension_semantics=("arbitrary",),
                                             vmem_limit_bytes=VMEM_LIMIT),
        name="layer0_gating_mlp",
    )(h, mix_g, w_in, ln_g, ln_b, w_s, b_st, w_out, mlp_g, w_up, w_down)


def _qkv_chain(sub, h_ref, g_ref, w_ref, out_refs, xs_ref):
    n_col = D_MODEL // LANES
    xn = _rms_norm(h_ref[sub * SUB_TILE:(sub + 1) * SUB_TILE, :], g_ref[...])
    for c in range(n_col):
        xs_ref[sub, c] = xn[:, c * LANES:(c + 1) * LANES]
    for grp, (_, dil) in enumerate(WINDOW_DILATIONS):
        per = SUB_TILE // dil
        if dil == 1:
            x = xn
        else:
            x = jnp.concatenate(
                [jnp.concatenate([xs_ref[sub, c, pl.ds(r, per, stride=dil), :] for c in range(n_col)],
                                 axis=1) for r in range(dil)], axis=0)
        x = x.astype(BF16)
        for part in range(3):
            j = part * N_DIL_GROUPS + grp
            y = _dot(x, w_ref[:, j * ATT_WIDTH:(j + 1) * ATT_WIDTH])
            if part == 0:
                y = y * (HEAD_DIM ** -0.5 * LOG2E)
            y = y.astype(BF16)
            o_ref = out_refs[j]
            for r in range(dil):
                o_ref[sub * per:(sub + 1) * per, r * ATT_WIDTH:(r + 1) * ATT_WIDTH] = y[r * per:(r + 1) * per, :]
            yield


def _qkv_kernel(h_ref, g_ref, w_ref, *refs):
    for sub in range(h_ref.shape[0] // SUB_TILE):
        for _ in _qkv_chain(sub, h_ref, g_ref, w_ref, refs[:-1], refs[-1]):
            pass


def _qkv(h, g, w_qkv):
    t = h.shape[0]
    tile = pl.BlockSpec((TOKEN_TILE, D_MODEL), lambda i: (i, 0))
    out_specs, out_shape = [], []
    for _ in range(3):
        for _, dil in WINDOW_DILATIONS:
            out_specs.append(pl.BlockSpec((TOKEN_TILE // dil, dil * ATT_WIDTH), lambda i: (i, 0)))
            out_shape.append(jax.ShapeDtypeStruct((t // dil, dil * ATT_WIDTH), BF16))
    return pl.pallas_call(
        _qkv_kernel,
        grid=(t // TOKEN_TILE,),
        in_specs=[tile, _whole((1, D_MODEL)), _whole(w_qkv.shape)],
        out_specs=out_specs,
        out_shape=out_shape,
        scratch_shapes=[pltpu.VMEM((TOKEN_TILE // SUB_TILE, D_MODEL // LANES, SUB_TILE, LANES), F32)],
        compiler_params=pltpu.CompilerParams(dimension_semantics=("arbitrary",),
                                             vmem_limit_bytes=VMEM_LIMIT),
        name="qkv_proj",
    )(h, g, w_qkv)


def _t5_bucket(distance):
    small = distance < MAX_EXACT
    nf = jnp.maximum(distance, 1).astype(F32)
    large = MAX_EXACT + (jnp.log(nf / MAX_EXACT) / math.log(REL_MAX_DISTANCE / MAX_EXACT)
                         * (N_BUCKETS - MAX_EXACT)).astype(jnp.int32)
    large = jnp.minimum(large, N_BUCKETS - 1)
    return jnp.where(small, distance, large)


def _bias_table_kernel(rb_ref, bucket_ref, out_ref):
    g = pl.program_id(0)
    bucket = bucket_ref[...]
    i_idx = lax.broadcasted_iota(jnp.int32, (BLK, 2 * BLK), 0)
    j_idx = lax.broadcasted_iota(jnp.int32, (BLK, 2 * BLK), 1)
    rel = BLK + i_idx - j_idx
    band = (rel >= 0) & (rel <= BLK)
    for h in range(ATT_HEADS):
        acc = jnp.zeros((BLK, 2 * BLK), F32)
        for b in range(N_BUCKETS):
            acc = jnp.where(bucket == b, rb_ref[b, g * ATT_HEADS + h], acc)
        rows = slice((h % HEADS_PER_VREG) * BLK, (h % HEADS_PER_VREG + 1) * BLK)
        out_ref[h // HEADS_PER_VREG, rows, :] = jnp.where(band, acc * LOG2E, NEG_INF)


def _bias_tables(rel_bias):
    rel = BLK + jnp.arange(BLK)[:, None] - jnp.arange(2 * BLK)[None, :]
    buckets = jnp.stack([_t5_bucket(jnp.clip(rel, 0, BLK) * dil) for _, dil in WINDOW_DILATIONS])
    shape = (N_DIL_GROUPS, HEAD_PAIRS, HEADS_PER_VREG * BLK, 2 * BLK)
    return pl.pallas_call(
        _bias_table_kernel,
        grid=(N_DIL_GROUPS,),
        in_specs=[pl.BlockSpec(memory_space=pltpu.SMEM),
                  pl.BlockSpec((None, BLK, 2 * BLK), lambda g: (g, 0, 0))],
        out_specs=pl.BlockSpec((None,) + shape[1:], lambda g: (g, 0, 0, 0)),
        out_shape=jax.ShapeDtypeStruct(shape, F32),
        name="rel_bias_tables",
    )(rel_bias.astype(F32), buckets.astype(jnp.int32))


def _lse_lane(head):
    return (head % HEADS_PER_VREG) * HEAD_DIM + HEADS_PER_VREG * (head // HEADS_PER_VREG)


def _attn_kernel(*refs, whole_sequence):
    if whole_sequence:
        q_ref, k_ref, v_ref, tb_ref, o_ref, lse_ref = refs
    else:
        q_ref, k_ref, v_ref, kp_ref, vp_ref, tb_ref, o_ref, lse_ref = refs
    rows = q_ref.shape[0]

    def unit(q, keys, vals, penalty):
        lane = lax.broadcasted_iota(jnp.int32, (BLK, LANES), 1)
        first = lane < HEAD_DIM
        ones = jnp.ones((2 * BLK, LANES), BF16)
        outs = []
        lse = jnp.zeros((BLK, LANES), F32)
        for pair in range(HEAD_PAIRS):
            sl = slice(pair * LANES, (pair + 1) * LANES)
            qp, kp, vp = q[:, sl], keys[:, sl], vals[:, sl]
            zero = jnp.zeros_like(qp)
            qq = jnp.concatenate([jnp.where(first, qp, zero), jnp.where(first, zero, qp)], axis=0)
            logits = _dot_nt(qq, kp) + tb_ref[pair]
            if penalty is not None:
                logits = logits + penalty
            m = jnp.max(logits, axis=-1, keepdims=True)
            p = jnp.exp2(logits - m).astype(BF16)
            res = _dot(p, jnp.concatenate([vp, ones], axis=1))
            num = jnp.where(first, res[:BLK, :LANES], res[BLK:, :LANES])
            den = jnp.where(first, res[:BLK, LANES:], res[BLK:, LANES:])
            outs.append((num / den).astype(BF16))
            lse_pair = jnp.where(first, m[:BLK], m[BLK:]) + jnp.log2(den)
            keep = (lane == _lse_lane(pair * HEADS_PER_VREG)) | (lane == _lse_lane(pair * HEADS_PER_VREG + 1))
            lse = jnp.where(keep, lse_pair, lse)
        return jnp.concatenate(outs, axis=1), lse

    at_start = True if whole_sequence else pl.program_id(2) == 0
    j_idx = lax.broadcasted_iota(jnp.int32, (2 * BLK, 2 * BLK), 1)
    penalty = jnp.where((j_idx < BLK) & at_start, NEG_INF, 0.0)
    for s in range(q_ref.shape[1] // ATT_WIDTH):
        cols = slice(s * ATT_WIDTH, (s + 1) * ATT_WIDTH)
        lse_cols = slice(s * LANES, (s + 1) * LANES)
        for u in range(rows // BLK):
            cur_rows = slice(u * BLK, (u + 1) * BLK)
            if u > 0:
                both = slice((u - 1) * BLK, (u + 1) * BLK)
                o, lse = unit(q_ref[cur_rows, cols], k_ref[both, cols], v_ref[both, cols], None)
            else:
                k_before = k_ref[cur_rows, cols] if whole_sequence else kp_ref[:, cols]
                v_before = v_ref[cur_rows, cols] if whole_sequence else vp_ref[:, cols]
                keys = jnp.concatenate([k_before, k_ref[cur_rows, cols]], axis=0)
                vals = jnp.concatenate([v_before, v_ref[cur_rows, cols]], axis=0)
                o, lse = unit(q_ref[cur_rows, cols], keys, vals, penalty)
            o_ref[cur_rows, cols] = o
            lse_ref[cur_rows, lse_cols] = lse


def _attention_group(q, k, v, table, batch, seq, dil):
    sub = seq // dil
    w = ATT_WIDTH
    rows = min(ATTN_BLOCKS_PER_STEP * BLK, sub)
    n_seq = ATTN_BLOCKS_PER_STEP * BLK // rows
    assert sub % rows == 0 and dil % n_seq == 0
    whole_sequence = rows == sub
    q, k, v = (a.reshape(batch, sub, dil * w) for a in (q, k, v))
    blocks_per_step = rows // BLK
    cur = pl.BlockSpec((None, rows, n_seq * w), lambda b, r, i: (b, i, r))
    prev = pl.BlockSpec((None, BLK, n_seq * w),
                        lambda b, r, i: (b, jnp.maximum(i * blocks_per_step - 1, 0), r))
    operands = (q, k, v) if whole_sequence else (q, k, v, k, v)
    in_specs = [cur, cur, cur] if whole_sequence else [cur, cur, cur, prev, prev]
    o, lse = pl.pallas_call(
        functools.partial(_attn_kernel, whole_sequence=whole_sequence),
        grid=(batch, dil // n_seq, sub // rows),
        in_specs=in_specs + [_whole(table.shape)],
        out_specs=[cur, pl.BlockSpec((None, rows, n_seq * LANES), lambda b, r, i: (b, i, r))],
        out_shape=[jax.ShapeDtypeStruct((batch, sub, dil * w), BF16),
                   jax.ShapeDtypeStruct((batch, sub, dil * LANES), F32)],
        compiler_params=pltpu.CompilerParams(
            dimension_semantics=("arbitrary", "arbitrary", "arbitrary"),
            vmem_limit_bytes=VMEM_LIMIT),
        name=f"dilated_attention_d{dil}",
    )(*operands, table)
    return o.reshape(batch * sub, dil * w), lse.reshape(batch * sub, dil * LANES)


def _natural_order(ref, scr_ref, sub, dil, width):
    per = SUB_TILE // dil
    rows = slice(sub * per, (sub + 1) * per)
    if dil == 1:
        return ref[rows, :].astype(F32)
    n_col = width // LANES
    for r in range(dil):
        for c in range(n_col):
            lanes = slice(r * width + c * LANES, r * width + (c + 1) * LANES)
            scr_ref[c, pl.ds(r, per, stride=dil), :] = ref[rows, lanes].astype(F32)
    return jnp.concatenate([scr_ref[c] for c in range(n_col)], axis=1)


def _layer1_tail_chain(sub, expand, h_ref, o_refs, l_refs, wout_ref, mlpg_ref, wup_ref, wdn_ref,
                       fing_ref, out_ref, os_ref, ls_ref):
    rows = slice(sub * SUB_TILE, (sub + 1) * SUB_TILE)
    dils = [d for _, d in WINDOW_DILATIONS]
    lses = [_natural_order(l_ref, None if g == 0 else ls_ref.at[sub, g - 1], sub, dils[g], LANES)
            for g, l_ref in enumerate(l_refs)]
    outs = [_natural_order(o_ref, None if g == 0 else os_ref.at[sub, g - 1], sub, dils[g], ATT_WIDTH)
            for g, o_ref in enumerate(o_refs)]
    m_all = jnp.maximum(jnp.maximum(lses[0], lses[1]), lses[2])
    num = jnp.zeros(outs[0].shape, F32)
    den = jnp.zeros(outs[0].shape, F32)
    for l, og in zip(lses, outs):
        wide = _dot(jnp.exp2(l - m_all).astype(BF16), expand)
        num = num + wide * og
        den = den + wide
    yield
    h1 = h_ref[rows, :] + _dot((num / den).astype(BF16), wout_ref[...])
    yield
    h2 = yield from _mlp_residual(h1, mlpg_ref[...], wup_ref, wdn_ref)
    out_ref[rows, :] = _rms_norm(h2, fing_ref[...])


def _layer1_tail_kernel(h_ref, o0_ref, o1_ref, o2_ref, l0_ref, l1_ref, l2_ref, *refs):
    row = lax.broadcasted_iota(jnp.int32, (LANES, ATT_WIDTH), 0)
    head = lax.broadcasted_iota(jnp.int32, (LANES, ATT_WIDTH), 1) // HEAD_DIM
    src = (head % HEADS_PER_VREG) * HEAD_DIM + HEADS_PER_VREG * (head // HEADS_PER_VREG)
    expand = jnp.where(row == src, 1.0, 0.0).astype(BF16)
    _interleave((_layer1_tail_chain(sub, expand, h_ref, (o0_ref, o1_ref, o2_ref), (l0_ref, l1_ref, l2_ref), *refs)
                 for sub in range(h_ref.shape[0] // SUB_TILE)), CHAIN_LAG)


def _layer1_tail(h, os_, lses, w_out, mlp_g, w_up, w_down, fin_g, layer):
    t = h.shape[0]
    tile = pl.BlockSpec((TOKEN_TILE, D_MODEL), lambda i: (i, 0))
    otiles = [pl.BlockSpec((TOKEN_TILE // d, d * ATT_WIDTH), lambda i: (i, 0)) for _, d in WINDOW_DILATIONS]
    ltiles = [pl.BlockSpec((TOKEN_TILE // d, d * LANES), lambda i: (i, 0)) for _, d in WINDOW_DILATIONS]
    return pl.pallas_call(
        _layer1_tail_kernel,
        grid=(t // TOKEN_TILE,),
        in_specs=[tile, *otiles, *ltiles, _whole(w_out.shape),
                  _whole((1, D_MODEL)), _layer_of(w_up, layer), _layer_of(w_down, layer),
                  _whole((1, D_MODEL))],
        out_specs=tile,
        out_shape=jax.ShapeDtypeStruct((t, D_MODEL), F32),
        scratch_shapes=[
            pltpu.VMEM((TOKEN_TILE // SUB_TILE, N_DIL_GROUPS - 1, ATT_WIDTH // LANES, SUB_TILE, LANES), F32),
            pltpu.VMEM((TOKEN_TILE // SUB_TILE, N_DIL_GROUPS - 1, 1, SUB_TILE, LANES), F32)],
        compiler_params=pltpu.CompilerParams(dimension_semantics=("arbitrary",),
                                             vmem_limit_bytes=VMEM_LIMIT),
        name="layer1_merge_proj_mlp",
    )(h, *os_, *lses, w_out, mlp_g, w_up, w_down, fin_g)


def kernel(x, mix_norm_g, mlp_norm_g, final_norm_g, a_w_in, a_ln_g, a_ln_b, a_w_s, a_b_s, a_w_out,
           b_w_qkv, b_w_out, rel_bias, w_up, w_down):
    batch, seq, d = x.shape
    max_dil = WINDOW_DILATIONS[-1][1]
    assert d == D_MODEL and seq % (BLK * max_dil) == 0
    assert seq % TOKEN_TILE == 0 and TOKEN_TILE % SUB_TILE == 0
    assert SUB_TILE % CHUNK == 0 and SUB_TILE % (16 * max_dil) == 0
    row = lambda a: a.reshape(1, -1).astype(F32)
    h = x.reshape(batch * seq, d)

    w_up, w_down = w_up.astype(BF16), w_down.astype(BF16)
    h = _layer0(h, row(mix_norm_g[0]), a_w_in[0].astype(BF16), row(a_ln_g[0]), row(a_ln_b[0]),
                a_w_s[0], a_b_s[0].T, a_w_out[0].astype(BF16), row(mlp_norm_g[0]), w_up, w_down, 0)

    qkv = _qkv(h, row(mix_norm_g[1]), b_w_qkv[0].astype(BF16))
    tables = _bias_tables(rel_bias)
    os_, lses = [], []
    for g, (_, dil) in enumerate(WINDOW_DILATIONS):
        o, lse = _attention_group(qkv[g], qkv[N_DIL_GROUPS + g], qkv[2 * N_DIL_GROUPS + g],
                                  tables[g], batch, seq, dil)
        os_.append(o)
        lses.append(lse)

    out = _layer1_tail(h, os_, lses, b_w_out[0].astype(BF16), row(mlp_norm_g[1]), w_up, w_down,
                       row(final_norm_g), 1)
    return out.reshape(batch, seq, d)
```

```python
import functools
import math

import jax
import jax.numpy as jnp
from jax import lax
from jax.experimental import pallas as pl
from jax.experimental.pallas import tpu as pltpu

D_MODEL = 1024
CHUNK = 128
GATE_WIDTH = D_MODEL
GATE_GROUPS = 8
WINDOW_DILATIONS = ((128, 1), (512, 4), (2048, 16))
N_DIL_GROUPS = len(WINDOW_DILATIONS)
ATT_HEADS = 8
HEAD_DIM = 64
ATT_WIDTH = ATT_HEADS * HEAD_DIM
N_BUCKETS = 32
MAX_EXACT = N_BUCKETS // 2
REL_MAX_DISTANCE = max(w for w, _ in WINDOW_DILATIONS)
D_FF = 4 * D_MODEL
EPS = 1e-6
NEG_INF = -1e30
LOG2E = math.log2(math.e)

BLK = 128
LANES = 128
HEADS_PER_VREG = LANES // HEAD_DIM
HEAD_PAIRS = ATT_HEADS // HEADS_PER_VREG
TOKEN_TILE = 1024
SUB_TILE = 512
CHAIN_LAG = 2
FF_CHUNK = 1024
ATTN_BLOCKS_PER_STEP = (32, 16, 32)
VMEM_LIMIT = 63 * 1024 * 1024

F32 = jnp.float32
BF16 = jnp.bfloat16


def _dot(a, b):
    return jnp.dot(a, b, preferred_element_type=F32)


def _dot_nt(a, b):
    return lax.dot_general(a, b, (((1,), (1,)), ((), ())), preferred_element_type=F32)


def _rms_norm(x, g):
    return x * lax.rsqrt(jnp.mean(x * x, axis=-1, keepdims=True) + EPS) * g


def _gelu_exact(x):
    return 0.5 * x * (1.0 + lax.erf(x * math.sqrt(0.5)))


def _interleave(chains, lag):
    chains = list(chains)
    alive = [True] * len(chains)
    t = 0
    while any(alive):
        for i, chain in enumerate(chains):
            if alive[i] and t >= i * lag:
                try:
                    next(chain)
                except StopIteration:
                    alive[i] = False
        t += 1


def _mlp_residual(h, g, wup_ref, wdn_ref):
    xn = _rms_norm(h, g).astype(BF16)
    acc = h
    for c in range(D_FF // FF_CHUNK):
        cols = slice(c * FF_CHUNK, (c + 1) * FF_CHUNK)
        up = _dot(xn, wup_ref[:, cols])
        yield
        act = jnp.square(jnp.maximum(up, 0.0)).astype(BF16)
        acc = acc + _dot(act, wdn_ref[cols, :])
        yield
    return acc


def _whole(shape):
    return pl.BlockSpec(shape, lambda *_: (0,) * len(shape))


def _layer_of(stacked, layer):
    return pl.BlockSpec((None,) + stacked.shape[1:], lambda *_: (layer, 0, 0),
                        pipeline_mode=pl.Buffered(1))


def _layer0_chain(rows, causal, h_ref, mixg_ref, win_ref, lng_ref, lnb_ref, ws_ref, bst_ref,
                  wout_ref, mlpg_ref, wup_ref, wdn_ref, out_ref):
    nc = SUB_TILE // CHUNK
    h = h_ref[rows, :]
    xn = _rms_norm(h, mixg_ref[...]).astype(BF16)
    v = _gelu_exact(_dot(xn, win_ref[:, GATE_WIDTH:]))
    yield
    u = _gelu_exact(_dot(xn, win_ref[:, :GATE_WIDTH]))
    yield
    mu = jnp.mean(v, axis=-1, keepdims=True)
    vc = v - mu
    vn = vc * lax.rsqrt(jnp.mean(vc * vc, axis=-1, keepdims=True) + EPS)
    vb = (vn * lng_ref[...] + lnb_ref[...]).astype(BF16)
    mixed = []
    for g in range(GATE_GROUPS):
        wg = jnp.where(causal, ws_ref[g], 0.0).astype(BF16)
        lanes = slice(g * CHUNK, (g + 1) * CHUNK)
        vg = jnp.concatenate([vb[c * CHUNK:(c + 1) * CHUNK, lanes] for c in range(nc)], axis=1)
        mixed.append(_dot(wg, vg) + bst_ref[:, g:g + 1])
    gate = jnp.concatenate(
        [jnp.concatenate([mixed[g][:, c * CHUNK:(c + 1) * CHUNK] for g in range(GATE_GROUPS)], axis=1)
         for c in range(nc)], axis=0)
    yield
    h1 = h + _dot((u * gate).astype(BF16), wout_ref[...])
    yield
    out_ref[rows, :] = yield from _mlp_residual(h1, mlpg_ref[...], wup_ref, wdn_ref)


def _layer0_kernel(h_ref, *refs):
    t_idx = lax.broadcasted_iota(jnp.int32, (CHUNK, CHUNK), 0)
    s_idx = lax.broadcasted_iota(jnp.int32, (CHUNK, CHUNK), 1)
    causal = s_idx <= t_idx
    _interleave((_layer0_chain(slice(sub * SUB_TILE, (sub + 1) * SUB_TILE), causal, h_ref, *refs)
                 for sub in range(h_ref.shape[0] // SUB_TILE)), CHAIN_LAG)


def _layer0(h, mix_g, w_in, ln_g, ln_b, w_s, b_st, w_out, mlp_g, w_up, w_down, layer):
    t = h.shape[0]
    tile = pl.BlockSpec((TOKEN_TILE, D_MODEL), lambda i: (i, 0))
    return pl.pallas_call(
        _layer0_kernel,
        grid=(t // TOKEN_TILE,),
        in_specs=[tile, _whole((1, D_MODEL)), _whole(w_in.shape), _whole((1, GATE_WIDTH)),
                  _whole((1, GATE_WIDTH)), _whole(w_s.shape), _whole(b_st.shape), _whole(w_out.shape),
                  _whole((1, D_MODEL)), _layer_of(w_up, layer), _layer_of(w_down, layer)],
        out_specs=tile,
        out_shape=jax.ShapeDtypeStruct((t, D_MODEL), F32),
        compiler_params=pltpu.CompilerParams(dimension_semantics=("arbitrary",),
                                             vmem_limit_bytes=VMEM_LIMIT),
        name="layer0_gating_mlp",
    )(h, mix_g, w_in, ln_g, ln_b, w_s, b_st, w_out, mlp_g, w_up, w_down)


def _qkv_chain(sub, h_ref, g_ref, w_ref, out_refs, xs_ref):
    n_col = D_MODEL // LANES
    xn = _rms_norm(h_ref[sub * SUB_TILE:(sub + 1) * SUB_TILE, :], g_ref[...])
    for c in range(n_col):
        xs_ref[sub, c] = xn[:, c * LANES:(c + 1) * LANES]
    for grp, (_, dil) in enumerate(WINDOW_DILATIONS):
        per = SUB_TILE // dil
        if dil == 1:
            x = xn
        else:
            x = jnp.concatenate(
                [jnp.concatenate([xs_ref[sub, c, pl.ds(r, per, stride=dil), :] for c in range(n_col)],
                                 axis=1) for r in range(dil)], axis=0)
        x = x.astype(BF16)
        for part in range(3):
            j = part * N_DIL_GROUPS + grp
            y = _dot(x, w_ref[:, j * ATT_WIDTH:(j + 1) * ATT_WIDTH])
            if part == 0:
                y = y * (HEAD_DIM ** -0.5 * LOG2E)
            y = y.astype(BF16)
            o_ref = out_refs[j]
            for r in range(dil):
                o_ref[sub * per:(sub + 1) * per, r * ATT_WIDTH:(r + 1) * ATT_WIDTH] = y[r * per:(r + 1) * per, :]
            yield


def _qkv_kernel(h_ref, g_ref, w_ref, *refs):
    for sub in range(h_ref.shape[0] // SUB_TILE):
        for _ in _qkv_chain(sub, h_ref, g_ref, w_ref, refs[:-1], refs[-1]):
            pass


def _qkv(h, g, w_qkv):
    t = h.shape[0]
    tile = pl.BlockSpec((TOKEN_TILE, D_MODEL), lambda i: (i, 0))
    out_specs, out_shape = [], []
    for _ in range(3):
        for _, dil in WINDOW_DILATIONS:
            out_specs.append(pl.BlockSpec((TOKEN_TILE // dil, dil * ATT_WIDTH), lambda i: (i, 0)))
            out_shape.append(jax.ShapeDtypeStruct((t // dil, dil * ATT_WIDTH), BF16))
    return pl.pallas_call(
        _qkv_kernel,
        grid=(t // TOKEN_TILE,),
        in_specs=[tile, _whole((1, D_MODEL)), _whole(w_qkv.shape)],
        out_specs=out_specs,
        out_shape=out_shape,
        scratch_shapes=[pltpu.VMEM((TOKEN_TILE // SUB_TILE, D_MODEL // LANES, SUB_TILE, LANES), F32)],
        compiler_params=pltpu.CompilerParams(dimension_semantics=("arbitrary",),
                                             vmem_limit_bytes=VMEM_LIMIT),
        name="qkv_proj",
    )(h, g, w_qkv)


def _t5_bucket(distance):
    small = distance < MAX_EXACT
    nf = jnp.maximum(distance, 1).astype(F32)
    large = MAX_EXACT + (jnp.log(nf / MAX_EXACT) / math.log(REL_MAX_DISTANCE / MAX_EXACT)
                         * (N_BUCKETS - MAX_EXACT)).astype(jnp.int32)
    large = jnp.minimum(large, N_BUCKETS - 1)
    return jnp.where(small, distance, large)


def _bias_table_kernel(rb_ref, bucket_ref, out_ref):
    g = pl.program_id(0)
    bucket = bucket_ref[...]
    i_idx = lax.broadcasted_iota(jnp.int32, (BLK, 2 * BLK), 0)
    j_idx = lax.broadcasted_iota(jnp.int32, (BLK, 2 * BLK), 1)
    rel = BLK + i_idx - j_idx
    band = (rel >= 0) & (rel <= BLK)
    for h in range(ATT_HEADS):
        acc = jnp.zeros((BLK, 2 * BLK), F32)
        for b in range(N_BUCKETS):
            acc = jnp.where(bucket == b, rb_ref[b, g * ATT_HEADS + h], acc)
        rows = slice((h % HEADS_PER_VREG) * BLK, (h % HEADS_PER_VREG + 1) * BLK)
        out_ref[0, h // HEADS_PER_VREG, rows, :] = jnp.where(band, acc * LOG2E, NEG_INF)
        out_ref[1, h // HEADS_PER_VREG, rows, :] = jnp.where(band & (j_idx >= BLK), acc * LOG2E, NEG_INF)


def _bias_tables(rel_bias):
    rel = BLK + jnp.arange(BLK)[:, None] - jnp.arange(2 * BLK)[None, :]
    buckets = jnp.stack([_t5_bucket(jnp.clip(rel, 0, BLK) * dil) for _, dil in WINDOW_DILATIONS])
    shape = (N_DIL_GROUPS, 2, HEAD_PAIRS, HEADS_PER_VREG * BLK, 2 * BLK)
    return pl.pallas_call(
        _bias_table_kernel,
        grid=(N_DIL_GROUPS,),
        in_specs=[pl.BlockSpec(memory_space=pltpu.SMEM),
                  pl.BlockSpec((None, BLK, 2 * BLK), lambda g: (g, 0, 0))],
        out_specs=pl.BlockSpec((None,) + shape[1:], lambda g: (g, 0, 0, 0, 0)),
        out_shape=jax.ShapeDtypeStruct(shape, F32),
        name="rel_bias_tables",
    )(rel_bias.astype(F32), buckets.astype(jnp.int32))


def _lse_lane(head):
    return (head % HEADS_PER_VREG) * HEAD_DIM + HEADS_PER_VREG * (head // HEADS_PER_VREG)


def _attn_kernel(*refs, whole_sequence):
    if whole_sequence:
        q_ref, k_ref, v_ref, tb_ref, o_ref, lse_ref = refs
    else:
        q_ref, k_ref, v_ref, kp_ref, vp_ref, tb_ref, o_ref, lse_ref = refs
    rows = q_ref.shape[0]

    def unit(q, keys, vals, variant):
        lane = lax.broadcasted_iota(jnp.int32, (BLK, LANES), 1)
        first = lane < HEAD_DIM
        ones = jnp.ones((2 * BLK, LANES), BF16)
        outs = []
        lse = jnp.zeros((BLK, LANES), F32)
        for pair in range(HEAD_PAIRS):
            sl = slice(pair * LANES, (pair + 1) * LANES)
            qp, kp, vp = q[:, sl], keys[:, sl], vals[:, sl]
            zero = jnp.zeros_like(qp)
            qq = jnp.concatenate([jnp.where(first, qp, zero), jnp.where(first, zero, qp)], axis=0)
            logits = _dot_nt(qq, kp) + tb_ref[variant, pair]
            m = jnp.max(logits, axis=-1, keepdims=True)
            p = jnp.exp2(logits - m).astype(BF16)
            res = _dot(p, jnp.concatenate([vp, ones], axis=1))
            num = jnp.where(first, res[:BLK, :LANES], res[BLK:, :LANES])
            den = jnp.where(first, res[:BLK, LANES:], res[BLK:, LANES:])
            outs.append((num / den).astype(BF16))
            lse_pair = jnp.where(first, m[:BLK], m[BLK:]) + jnp.log2(den)
            keep = (lane == _lse_lane(pair * HEADS_PER_VREG)) | (lane == _lse_lane(pair * HEADS_PER_VREG + 1))
            lse = jnp.where(keep, lse_pair, lse)
        return jnp.concatenate(outs, axis=1), lse

    first_variant = 1 if whole_sequence else jnp.where(pl.program_id(2) == 0, 1, 0)
    for s in range(q_ref.shape[1] // ATT_WIDTH):
        cols = slice(s * ATT_WIDTH, (s + 1) * ATT_WIDTH)
        lse_cols = slice(s * LANES, (s + 1) * LANES)
        for u in range(rows // BLK):
            cur_rows = slice(u * BLK, (u + 1) * BLK)
            if u > 0:
                both = slice((u - 1) * BLK, (u + 1) * BLK)
                o, lse = unit(q_ref[cur_rows, cols], k_ref[both, cols], v_ref[both, cols], 0)
            else:
                k_before = k_ref[cur_rows, cols] if whole_sequence else kp_ref[:, cols]
                v_before = v_ref[cur_rows, cols] if whole_sequence else vp_ref[:, cols]
                keys = jnp.concatenate([k_before, k_ref[cur_rows, cols]], axis=0)
                vals = jnp.concatenate([v_before, v_ref[cur_rows, cols]], axis=0)
                o, lse = unit(q_ref[cur_rows, cols], keys, vals, first_variant)
            o_ref[cur_rows, cols] = o
            lse_ref[cur_rows, lse_cols] = lse


def _attention_group(q, k, v, table, batch, seq, dil, blocks):
    sub = seq // dil
    w = ATT_WIDTH
    rows = min(blocks * BLK, sub)
    n_seq = blocks * BLK // rows
    assert sub % rows == 0 and dil % n_seq == 0
    whole_sequence = rows == sub
    q, k, v = (a.reshape(batch, sub, dil * w) for a in (q, k, v))
    blocks_per_step = rows // BLK
    cur = pl.BlockSpec((None, rows, n_seq * w), lambda b, r, i: (b, i, r))
    prev = pl.BlockSpec((None, BLK, n_seq * w),
                        lambda b, r, i: (b, jnp.maximum(i * blocks_per_step - 1, 0), r))
    operands = (q, k, v) if whole_sequence else (q, k, v, k, v)
    in_specs = [cur, cur, cur] if whole_sequence else [cur, cur, cur, prev, prev]
    o, lse = pl.pallas_call(
        functools.partial(_attn_kernel, whole_sequence=whole_sequence),
        grid=(batch, dil // n_seq, sub // rows),
        in_specs=in_specs + [_whole(table.shape)],
        out_specs=[cur, pl.BlockSpec((None, rows, n_seq * LANES), lambda b, r, i: (b, i, r))],
        out_shape=[jax.ShapeDtypeStruct((batch, sub, dil * w), BF16),
                   jax.ShapeDtypeStruct((batch, sub, dil * LANES), F32)],
        compiler_params=pltpu.CompilerParams(
            dimension_semantics=("arbitrary", "arbitrary", "arbitrary"),
            vmem_limit_bytes=VMEM_LIMIT),
        name=f"dilated_attention_d{dil}",
    )(*operands, table)
    return o.reshape(batch * sub, dil * w), lse.reshape(batch * sub, dil * LANES)


def _natural_order(ref, scr_ref, sub, dil, width):
    per = SUB_TILE // dil
    rows = slice(sub * per, (sub + 1) * per)
    if dil == 1:
        return ref[rows, :].astype(F32)
    n_col = width // LANES
    for r in range(dil):
        for c in range(n_col):
            lanes = slice(r * width + c * LANES, r * width + (c + 1) * LANES)
            scr_ref[c, pl.ds(r, per, stride=dil), :] = ref[rows, lanes].astype(F32)
    return jnp.concatenate([scr_ref[c] for c in range(n_col)], axis=1)


def _layer1_tail_chain(sub, expand, h_ref, o_refs, l_refs, wout_ref, mlpg_ref, wup_ref, wdn_ref,
                       fing_ref, out_ref, os_ref, ls_ref):
    rows = slice(sub * SUB_TILE, (sub + 1) * SUB_TILE)
    dils = [d for _, d in WINDOW_DILATIONS]
    lses = [_natural_order(l_ref, None if g == 0 else ls_ref.at[sub, g - 1], sub, dils[g], LANES)
            for g, l_ref in enumerate(l_refs)]
    outs = [_natural_order(o_ref, None if g == 0 else os_ref.at[sub, g - 1], sub, dils[g], ATT_WIDTH)
            for g, o_ref in enumerate(o_refs)]
    m_all = jnp.maximum(jnp.maximum(lses[0], lses[1]), lses[2])
    num = jnp.zeros(outs[0].shape, F32)
    den = jnp.zeros(outs[0].shape, F32)
    for l, og in zip(lses, outs):
        wide = _dot(jnp.exp2(l - m_all).astype(BF16), expand)
        num = num + wide * og
        den = den + wide
    yield
    h1 = h_ref[rows, :] + _dot((num / den).astype(BF16), wout_ref[...])
    yield
    h2 = yield from _mlp_residual(h1, mlpg_ref[...], wup_ref, wdn_ref)
    out_ref[rows, :] = _rms_norm(h2, fing_ref[...])


def _layer1_tail_kernel(h_ref, o0_ref, o1_ref, o2_ref, l0_ref, l1_ref, l2_ref, *refs):
    row = lax.broadcasted_iota(jnp.int32, (LANES, ATT_WIDTH), 0)
    head = lax.broadcasted_iota(jnp.int32, (LANES, ATT_WIDTH), 1) // HEAD_DIM
    src = (head % HEADS_PER_VREG) * HEAD_DIM + HEADS_PER_VREG * (head // HEADS_PER_VREG)
    expand = jnp.where(row == src, 1.0, 0.0).astype(BF16)
    _interleave((_layer1_tail_chain(sub, expand, h_ref, (o0_ref, o1_ref, o2_ref), (l0_ref, l1_ref, l2_ref), *refs)
                 for sub in range(h_ref.shape[0] // SUB_TILE)), CHAIN_LAG)


def _layer1_tail(h, os_, lses, w_out, mlp_g, w_up, w_down, fin_g, layer):
    t = h.shape[0]
    tile = pl.BlockSpec((TOKEN_TILE, D_MODEL), lambda i: (i, 0))
    otiles = [pl.BlockSpec((TOKEN_TILE // d, d * ATT_WIDTH), lambda i: (i, 0)) for _, d in WINDOW_DILATIONS]
    ltiles = [pl.BlockSpec((TOKEN_TILE // d, d * LANES), lambda i: (i, 0)) for _, d in WINDOW_DILATIONS]
    return pl.pallas_call(
        _layer1_tail_kernel,
        grid=(t // TOKEN_TILE,),
        in_specs=[tile, *otiles, *ltiles, _whole(w_out.shape),
                  _whole((1, D_MODEL)), _layer_of(w_up, layer), _layer_of(w_down, layer),
                  _whole((1, D_MODEL))],
        out_specs=tile,
        out_shape=jax.ShapeDtypeStruct((t, D_MODEL), F32),
        scratch_shapes=[
            pltpu.VMEM((TOKEN_TILE // SUB_TILE, N_DIL_GROUPS - 1, ATT_WIDTH // LANES, SUB_TILE, LANES), F32),
            pltpu.VMEM((TOKEN_TILE // SUB_TILE, N_DIL_GROUPS - 1, 1, SUB_TILE, LANES), F32)],
        compiler_params=pltpu.CompilerParams(dimension_semantics=("arbitrary",),
                                             vmem_limit_bytes=VMEM_LIMIT),
        name="layer1_merge_proj_mlp",
    )(h, *os_, *lses, w_out, mlp_g, w_up, w_down, fin_g)


def kernel(x, mix_norm_g, mlp_norm_g, final_norm_g, a_w_in, a_ln_g, a_ln_b, a_w_s, a_b_s, a_w_out,
           b_w_qkv, b_w_out, rel_bias, w_up, w_down):
    batch, seq, d = x.shape
    max_dil = WINDOW_DILATIONS[-1][1]
    assert d == D_MODEL and seq % (BLK * max_dil) == 0
    assert seq % TOKEN_TILE == 0 and TOKEN_TILE % SUB_TILE == 0
    assert SUB_TILE % CHUNK == 0 and SUB_TILE % (16 * max_dil) == 0
    row = lambda a: a.reshape(1, -1).astype(F32)
    h = x.reshape(batch * seq, d)

    w_up, w_down = w_up.astype(BF16), w_down.astype(BF16)
    h = _layer0(h, row(mix_norm_g[0]), a_w_in[0].astype(BF16), row(a_ln_g[0]), row(a_ln_b[0]),
                a_w_s[0], a_b_s[0].T, a_w_out[0].astype(BF16), row(mlp_norm_g[0]), w_up, w_down, 0)

    qkv = _qkv(h, row(mix_norm_g[1]), b_w_qkv[0].astype(BF16))
    tables = _bias_tables(rel_bias)
    os_, lses = [], []
    for g, (_, dil) in enumerate(WINDOW_DILATIONS):
        o, lse = _attention_group(qkv[g], qkv[N_DIL_GROUPS + g], qkv[2 * N_DIL_GROUPS + g],
                                  tables[g], batch, seq, dil, ATTN_BLOCKS_PER_STEP[g])
        os_.append(o)
        lses.append(lse)

    out = _layer1_tail(h, os_, lses, b_w_out[0].astype(BF16), row(mlp_norm_g[1]), w_up, w_down,
                       row(final_norm_g), 1)
    return out.reshape(batch, seq, d)
```

```python
import functools
import math

import jax
import jax.numpy as jnp
from jax import lax
from jax.experimental import pallas as pl
from jax.experimental.pallas import tpu as pltpu

D_MODEL = 1024
CHUNK = 128
GATE_WIDTH = D_MODEL
GATE_GROUPS = 8
WINDOW_DILATIONS = ((128, 1), (512, 4), (2048, 16))
N_DIL_GROUPS = len(WINDOW_DILATIONS)
ATT_HEADS = 8
HEAD_DIM = 64
ATT_WIDTH = ATT_HEADS * HEAD_DIM
N_BUCKETS = 32
MAX_EXACT = N_BUCKETS // 2
REL_MAX_DISTANCE = max(w for w, _ in WINDOW_DILATIONS)
D_FF = 4 * D_MODEL
EPS = 1e-6
NEG_INF = -1e30
LOG2E = math.log2(math.e)

BLK = 128
LANES = 128
HEADS_PER_VREG = LANES // HEAD_DIM
HEAD_PAIRS = ATT_HEADS // HEADS_PER_VREG
TOKEN_TILE = 1024
SUB_TILE = 512
CHAIN_LAG = 2
FF_CHUNK = 1024
ATTN_BLOCKS_PER_STEP = 16
VMEM_LIMIT = 63 * 1024 * 1024

F32 = jnp.float32
BF16 = jnp.bfloat16


def _dot(a, b):
    return jnp.dot(a, b, preferred_element_type=F32)


def _dot_nt(a, b):
    return lax.dot_general(a, b, (((1,), (1,)), ((), ())), preferred_element_type=F32)


def _rms_norm(x, g):
    return x * lax.rsqrt(jnp.mean(x * x, axis=-1, keepdims=True) + EPS) * g


def _gelu_exact(x):
    return 0.5 * x * (1.0 + lax.erf(x * math.sqrt(0.5)))


def _interleave(chains, lag):
    chains = list(chains)
    alive = [True] * len(chains)
    t = 0
    while any(alive):
        for i, chain in enumerate(chains):
            if alive[i] and t >= i * lag:
                try:
                    next(chain)
                except StopIteration:
                    alive[i] = False
        t += 1


def _mlp_residual(h, g, wup_ref, wdn_ref):
    xn = _rms_norm(h, g).astype(BF16)
    acc = h
    for c in range(D_FF // FF_CHUNK):
        cols = slice(c * FF_CHUNK, (c + 1) * FF_CHUNK)
        up = _dot(xn, wup_ref[:, cols])
        yield
        act = jnp.square(jnp.maximum(up, 0.0)).astype(BF16)
        acc = acc + _dot(act, wdn_ref[cols, :])
        yield
    return acc


def _whole(shape):
    return pl.BlockSpec(shape, lambda *_: (0,) * len(shape))


def _cast_rider_specs(mats, n_steps):
    in_specs, out_specs, out_shape = [], [], []
    for arr, layer in mats:
        k, n = arr.shape[-2:]
        rows = k // n_steps
        assert k % n_steps == 0 and rows % 16 == 0
        in_specs.append(pl.BlockSpec((None, rows, n), lambda i, layer=layer: (layer, i, 0)))
        out_specs.append(pl.BlockSpec((rows, n), lambda i: (i, 0)))
        out_shape.append(jax.ShapeDtypeStruct((k, n), BF16))
    return in_specs, out_specs, out_shape


def _cast_riders(src_refs, dst_refs):
    for src, dst in zip(src_refs, dst_refs):
        dst[...] = src[...].astype(BF16)


def _layer0_chain(rows, causal, h_ref, mixg_ref, win_ref, lng_ref, lnb_ref, ws_ref, bst_ref,
                  wout_ref, mlpg_ref, wup_ref, wdn_ref, out_ref):
    nc = SUB_TILE // CHUNK
    h = h_ref[rows, :]
    xn = _rms_norm(h, mixg_ref[...]).astype(BF16)
    v = _gelu_exact(_dot(xn, win_ref[:, GATE_WIDTH:]))
    yield
    u = _gelu_exact(_dot(xn, win_ref[:, :GATE_WIDTH]))
    yield
    mu = jnp.mean(v, axis=-1, keepdims=True)
    vc = v - mu
    vn = vc * lax.rsqrt(jnp.mean(vc * vc, axis=-1, keepdims=True) + EPS)
    vb = (vn * lng_ref[...] + lnb_ref[...]).astype(BF16)
    mixed = []
    for g in range(GATE_GROUPS):
        wg = jnp.where(causal, ws_ref[g], 0.0).astype(BF16)
        lanes = slice(g * CHUNK, (g + 1) * CHUNK)
        vg = jnp.concatenate([vb[c * CHUNK:(c + 1) * CHUNK, lanes] for c in range(nc)], axis=1)
        mixed.append(_dot(wg, vg) + bst_ref[:, g:g + 1])
    gate = jnp.concatenate(
        [jnp.concatenate([mixed[g][:, c * CHUNK:(c + 1) * CHUNK] for g in range(GATE_GROUPS)], axis=1)
         for c in range(nc)], axis=0)
    yield
    h1 = h + _dot((u * gate).astype(BF16), wout_ref[...])
    yield
    out_ref[rows, :] = yield from _mlp_residual(h1, mlpg_ref[...], wup_ref, wdn_ref)


def _layer0_kernel(h_ref, *refs):
    *weights, cast_src, out_ref, cast_dst = refs
    t_idx = lax.broadcasted_iota(jnp.int32, (CHUNK, CHUNK), 0)
    s_idx = lax.broadcasted_iota(jnp.int32, (CHUNK, CHUNK), 1)
    causal = s_idx <= t_idx
    _interleave((_layer0_chain(slice(sub * SUB_TILE, (sub + 1) * SUB_TILE), causal, h_ref, *weights, out_ref)
                 for sub in range(h_ref.shape[0] // SUB_TILE)), CHAIN_LAG)
    _cast_riders([cast_src], [cast_dst])


def _layer0(h, mix_g, w_in, ln_g, ln_b, w_s, b_st, w_out, mlp_g, w_up, w_down, to_cast):
    t = h.shape[0]
    n_steps = t // TOKEN_TILE
    tile = pl.BlockSpec((TOKEN_TILE, D_MODEL), lambda i: (i, 0))
    cast_in, cast_out, cast_shape = _cast_rider_specs([to_cast], n_steps)
    return pl.pallas_call(
        _layer0_kernel,
        grid=(n_steps,),
        in_specs=[tile, _whole((1, D_MODEL)), _whole(w_in.shape), _whole((1, GATE_WIDTH)),
                  _whole((1, GATE_WIDTH)), _whole(w_s.shape), _whole(b_st.shape), _whole(w_out.shape),
                  _whole((1, D_MODEL)), _whole(w_up.shape), _whole(w_down.shape)] + cast_in,
        out_specs=[tile] + cast_out,
        out_shape=[jax.ShapeDtypeStruct((t, D_MODEL), F32)] + cast_shape,
        compiler_params=pltpu.CompilerParams(dimension_semantics=("arbitrary",),
                                             vmem_limit_bytes=VMEM_LIMIT),
        name="layer0_gating_mlp",
    )(h, mix_g, w_in, ln_g, ln_b, w_s, b_st, w_out, mlp_g, w_up, w_down, to_cast[0])


def _qkv_chain(sub, h_ref, g_ref, w_ref, out_refs, xs_ref):
    n_col = D_MODEL // LANES
    xn = _rms_norm(h_ref[sub * SUB_TILE:(sub + 1) * SUB_TILE, :], g_ref[...])
    for c in range(n_col):
        xs_ref[sub, c] = xn[:, c * LANES:(c + 1) * LANES]
    for grp, (_, dil) in enumerate(WINDOW_DILATIONS):
        per = SUB_TILE // dil
        if dil == 1:
            x = xn
        else:
            x = jnp.concatenate(
                [jnp.concatenate([xs_ref[sub, c, pl.ds(r, per, stride=dil), :] for c in range(n_col)],
                                 axis=1) for r in range(dil)], axis=0)
        x = x.astype(BF16)
        for part in range(3):
            j = part * N_DIL_GROUPS + grp
            y = _dot(x, w_ref[:, j * ATT_WIDTH:(j + 1) * ATT_WIDTH])
            if part == 0:
                y = y * (HEAD_DIM ** -0.5 * LOG2E)
            y = y.astype(BF16)
            o_ref = out_refs[j]
            for r in range(dil):
                o_ref[sub * per:(sub + 1) * per, r * ATT_WIDTH:(r + 1) * ATT_WIDTH] = y[r * per:(r + 1) * per, :]
            yield


def _qkv_kernel(h_ref, g_ref, w_ref, *refs, n_cast):
    n_out = 3 * N_DIL_GROUPS
    cast_src, out_refs = refs[:n_cast], refs[n_cast:n_cast + n_out]
    cast_dst, xs_ref = refs[n_cast + n_out:-1], refs[-1]
    for sub in range(h_ref.shape[0] // SUB_TILE):
        for _ in _qkv_chain(sub, h_ref, g_ref, w_ref, out_refs, xs_ref):
            pass
    _cast_riders(cast_src, cast_dst)


def _qkv(h, g, w_qkv, to_cast):
    t = h.shape[0]
    n_steps = t // TOKEN_TILE
    tile = pl.BlockSpec((TOKEN_TILE, D_MODEL), lambda i: (i, 0))
    cast_in, cast_out, cast_shape = _cast_rider_specs(to_cast, n_steps)
    out_specs, out_shape = [], []
    for _ in range(3):
        for _, dil in WINDOW_DILATIONS:
            out_specs.append(pl.BlockSpec((TOKEN_TILE // dil, dil * ATT_WIDTH), lambda i: (i, 0)))
            out_shape.append(jax.ShapeDtypeStruct((t // dil, dil * ATT_WIDTH), BF16))
    return pl.pallas_call(
        functools.partial(_qkv_kernel, n_cast=len(to_cast)),
        grid=(n_steps,),
        in_specs=[tile, _whole((1, D_MODEL)), _whole(w_qkv.shape)] + cast_in,
        out_specs=out_specs + cast_out,
        out_shape=out_shape + cast_shape,
        scratch_shapes=[pltpu.VMEM((TOKEN_TILE // SUB_TILE, D_MODEL // LANES, SUB_TILE, LANES), F32)],
        compiler_params=pltpu.CompilerParams(dimension_semantics=("arbitrary",),
                                             vmem_limit_bytes=VMEM_LIMIT),
        name="qkv_proj",
    )(h, g, w_qkv, *[arr for arr, _ in to_cast])


def _t5_bucket(distance):
    small = distance < MAX_EXACT
    nf = jnp.maximum(distance, 1).astype(F32)
    large = MAX_EXACT + (jnp.log(nf / MAX_EXACT) / math.log(REL_MAX_DISTANCE / MAX_EXACT)
                         * (N_BUCKETS - MAX_EXACT)).astype(jnp.int32)
    large = jnp.minimum(large, N_BUCKETS - 1)
    return jnp.where(small, distance, large)


def _bias_table_kernel(rb_ref, bucket_ref, out_ref):
    g = pl.program_id(0)
    bucket = bucket_ref[...]
    i_idx = lax.broadcasted_iota(jnp.int32, (BLK, 2 * BLK), 0)
    j_idx = lax.broadcasted_iota(jnp.int32, (BLK, 2 * BLK), 1)
    rel = BLK + i_idx - j_idx
    band = (rel >= 0) & (rel <= BLK)
    for h in range(ATT_HEADS):
        acc = jnp.zeros((BLK, 2 * BLK), F32)
        for b in range(N_BUCKETS):
            acc = jnp.where(bucket == b, rb_ref[b, g * ATT_HEADS + h], acc)
        rows = slice((h % HEADS_PER_VREG) * BLK, (h % HEADS_PER_VREG + 1) * BLK)
        out_ref[0, h // HEADS_PER_VREG, rows, :] = jnp.where(band, acc * LOG2E, NEG_INF)
        out_ref[1, h // HEADS_PER_VREG, rows, :] = jnp.where(band & (j_idx >= BLK), acc * LOG2E, NEG_INF)


def _bias_tables(rel_bias):
    rel = BLK + jnp.arange(BLK)[:, None] - jnp.arange(2 * BLK)[None, :]
    buckets = jnp.stack([_t5_bucket(jnp.clip(rel, 0, BLK) * dil) for _, dil in WINDOW_DILATIONS])
    shape = (N_DIL_GROUPS, 2, HEAD_PAIRS, HEADS_PER_VREG * BLK, 2 * BLK)
    return pl.pallas_call(
        _bias_table_kernel,
        grid=(N_DIL_GROUPS,),
        in_specs=[pl.BlockSpec(memory_space=pltpu.SMEM),
                  pl.BlockSpec((None, BLK, 2 * BLK), lambda g: (g, 0, 0))],
        out_specs=pl.BlockSpec((None,) + shape[1:], lambda g: (g, 0, 0, 0, 0)),
        out_shape=jax.ShapeDtypeStruct(shape, F32),
        name="rel_bias_tables",
    )(rel_bias.astype(F32), buckets.astype(jnp.int32))


def _lse_lane(head):
    return (head % HEADS_PER_VREG) * HEAD_DIM + HEADS_PER_VREG * (head // HEADS_PER_VREG)


def _attn_kernel(*refs, whole_sequence):
    if whole_sequence:
        q_ref, k_ref, v_ref, tb_ref, o_ref, lse_ref = refs
    else:
        q_ref, k_ref, v_ref, kp_ref, vp_ref, tb_ref, o_ref, lse_ref = refs
    rows = q_ref.shape[0]

    def unit(q, keys, vals, variant):
        lane = lax.broadcasted_iota(jnp.int32, (BLK, LANES), 1)
        first = lane < HEAD_DIM
        ones = jnp.ones((2 * BLK, LANES), BF16)
        outs = []
        lse = jnp.zeros((BLK, LANES), F32)
        for pair in range(HEAD_PAIRS):
            sl = slice(pair * LANES, (pair + 1) * LANES)
            qp, kp, vp = q[:, sl], keys[:, sl], vals[:, sl]
            zero = jnp.zeros_like(qp)
            qq = jnp.concatenate([jnp.where(first, qp, zero), jnp.where(first, zero, qp)], axis=0)
            logits = _dot_nt(qq, kp) + tb_ref[variant, pair]
            m = jnp.max(logits, axis=-1, keepdims=True)
            p = jnp.exp2(logits - m).astype(BF16)
            res = _dot(p, jnp.concatenate([vp, ones], axis=1))
            num = jnp.where(first, res[:BLK, :LANES], res[BLK:, :LANES])
            den = jnp.where(first, res[:BLK, LANES:], res[BLK:, LANES:])
            outs.append((num / den).astype(BF16))
            lse_pair = jnp.where(first, m[:BLK], m[BLK:]) + jnp.log2(den)
            keep = (lane == _lse_lane(pair * HEADS_PER_VREG)) | (lane == _lse_lane(pair * HEADS_PER_VREG + 1))
            lse = jnp.where(keep, lse_pair, lse)
        return jnp.concatenate(outs, axis=1), lse

    first_variant = 1 if whole_sequence else jnp.where(pl.program_id(2) == 0, 1, 0)
    for s in range(q_ref.shape[1] // ATT_WIDTH):
        cols = slice(s * ATT_WIDTH, (s + 1) * ATT_WIDTH)
        lse_cols = slice(s * LANES, (s + 1) * LANES)
        for u in range(rows // BLK):
            cur_rows = slice(u * BLK, (u + 1) * BLK)
            if u > 0:
                both = slice((u - 1) * BLK, (u + 1) * BLK)
                o, lse = unit(q_ref[cur_rows, cols], k_ref[both, cols], v_ref[both, cols], 0)
            else:
                k_before = k_ref[cur_rows, cols] if whole_sequence else kp_ref[:, cols]
                v_before = v_ref[cur_rows, cols] if whole_sequence else vp_ref[:, cols]
                keys = jnp.concatenate([k_before, k_ref[cur_rows, cols]], axis=0)
                vals = jnp.concatenate([v_before, v_ref[cur_rows, cols]], axis=0)
                o, lse = unit(q_ref[cur_rows, cols], keys, vals, first_variant)
            o_ref[cur_rows, cols] = o
            lse_ref[cur_rows, lse_cols] = lse


def _attention_group(q, k, v, table, batch, seq, dil, blocks):
    sub = seq // dil
    w = ATT_WIDTH
    rows = min(blocks * BLK, sub)
    n_seq = blocks * BLK // rows
    assert sub % rows == 0 and dil % n_seq == 0
    whole_sequence = rows == sub
    q, k, v = (a.reshape(batch, sub, dil * w) for a in (q, k, v))
    blocks_per_step = rows // BLK
    cur = pl.BlockSpec((None, rows, n_seq * w), lambda b, r, i: (b, i, r))
    prev = pl.BlockSpec((None, BLK, n_seq * w),
                        lambda b, r, i: (b, jnp.maximum(i * blocks_per_step - 1, 0), r))
    operands = (q, k, v) if whole_sequence else (q, k, v, k, v)
    in_specs = [cur, cur, cur] if whole_sequence else [cur, cur, cur, prev, prev]
    o, lse = pl.pallas_call(
        functools.partial(_attn_kernel, whole_sequence=whole_sequence),
        grid=(batch, dil // n_seq, sub // rows),
        in_specs=in_specs + [_whole(table.shape)],
        out_specs=[cur, pl.BlockSpec((None, rows, n_seq * LANES), lambda b, r, i: (b, i, r))],
        out_shape=[jax.ShapeDtypeStruct((batch, sub, dil * w), BF16),
                   jax.ShapeDtypeStruct((batch, sub, dil * LANES), F32)],
        compiler_params=pltpu.CompilerParams(
            dimension_semantics=("arbitrary", "arbitrary", "arbitrary"),
            vmem_limit_bytes=VMEM_LIMIT),
        name=f"dilated_attention_d{dil}",
    )(*operands, table)
    return o.reshape(batch * sub, dil * w), lse.reshape(batch * sub, dil * LANES)


def _natural_order(ref, scr_ref, sub, dil, width):
    per = SUB_TILE // dil
    rows = slice(sub * per, (sub + 1) * per)
    if dil == 1:
        return ref[rows, :].astype(F32)
    n_col = width // LANES
    for r in range(dil):
        for c in range(n_col):
            lanes = slice(r * width + c * LANES, r * width + (c + 1) * LANES)
            scr_ref[c, pl.ds(r, per, stride=dil), :] = ref[rows, lanes].astype(F32)
    return jnp.concatenate([scr_ref[c] for c in range(n_col)], axis=1)


def _layer1_tail_chain(sub, expand, h_ref, o_refs, l_refs, wout_ref, mlpg_ref, wup_ref, wdn_ref,
                       fing_ref, out_ref, os_ref, ls_ref):
    rows = slice(sub * SUB_TILE, (sub + 1) * SUB_TILE)
    dils = [d for _, d in WINDOW_DILATIONS]
    lses = [_natural_order(l_ref, None if g == 0 else ls_ref.at[sub, g - 1], sub, dils[g], LANES)
            for g, l_ref in enumerate(l_refs)]
    outs = [_natural_order(o_ref, None if g == 0 else os_ref.at[sub, g - 1], sub, dils[g], ATT_WIDTH)
            for g, o_ref in enumerate(o_refs)]
    m_all = jnp.maximum(jnp.maximum(lses[0], lses[1]), lses[2])
    num = jnp.zeros(outs[0].shape, F32)
    den = jnp.zeros(outs[0].shape, F32)
    for l, og in zip(lses, outs):
        wide = _dot(jnp.exp2(l - m_all).astype(BF16), expand)
        num = num + wide * og
        den = den + wide
    yield
    h1 = h_ref[rows, :] + _dot((num / den).astype(BF16), wout_ref[...])
    yield
    h2 = yield from _mlp_residual(h1, mlpg_ref[...], wup_ref, wdn_ref)
    out_ref[rows, :] = _rms_norm(h2, fing_ref[...])


def _layer1_tail_kernel(h_ref, o0_ref, o1_ref, o2_ref, l0_ref, l1_ref, l2_ref, *refs):
    row = lax.broadcasted_iota(jnp.int32, (LANES, ATT_WIDTH), 0)
    head = lax.broadcasted_iota(jnp.int32, (LANES, ATT_WIDTH), 1) // HEAD_DIM
    src = (head % HEADS_PER_VREG) * HEAD_DIM + HEADS_PER_VREG * (head // HEADS_PER_VREG)
    expand = jnp.where(row == src, 1.0, 0.0).astype(BF16)
    _interleave((_layer1_tail_chain(sub, expand, h_ref, (o0_ref, o1_ref, o2_ref), (l0_ref, l1_ref, l2_ref), *refs)
                 for sub in range(h_ref.shape[0] // SUB_TILE)), CHAIN_LAG)


def _layer1_tail(h, os_, lses, w_out, mlp_g, w_up, w_down, fin_g):
    t = h.shape[0]
    tile = pl.BlockSpec((TOKEN_TILE, D_MODEL), lambda i: (i, 0))
    otiles = [pl.BlockSpec((TOKEN_TILE // d, d * ATT_WIDTH), lambda i: (i, 0)) for _, d in WINDOW_DILATIONS]
    ltiles = [pl.BlockSpec((TOKEN_TILE // d, d * LANES), lambda i: (i, 0)) for _, d in WINDOW_DILATIONS]
    return pl.pallas_call(
        _layer1_tail_kernel,
        grid=(t // TOKEN_TILE,),
        in_specs=[tile, *otiles, *ltiles, _whole(w_out.shape),
                  _whole((1, D_MODEL)), _whole(w_up.shape), _whole(w_down.shape),
                  _whole((1, D_MODEL))],
        out_specs=tile,
        out_shape=jax.ShapeDtypeStruct((t, D_MODEL), F32),
        scratch_shapes=[
            pltpu.VMEM((TOKEN_TILE // SUB_TILE, N_DIL_GROUPS - 1, ATT_WIDTH // LANES, SUB_TILE, LANES), F32),
            pltpu.VMEM((TOKEN_TILE // SUB_TILE, N_DIL_GROUPS - 1, 1, SUB_TILE, LANES), F32)],
        compiler_params=pltpu.CompilerParams(dimension_semantics=("arbitrary",),
                                             vmem_limit_bytes=VMEM_LIMIT),
        name="layer1_merge_proj_mlp",
    )(h, *os_, *lses, w_out, mlp_g, w_up, w_down, fin_g)


def kernel(x, mix_norm_g, mlp_norm_g, final_norm_g, a_w_in, a_ln_g, a_ln_b, a_w_s, a_b_s, a_w_out,
           b_w_qkv, b_w_out, rel_bias, w_up, w_down):
    batch, seq, d = x.shape
    max_dil = WINDOW_DILATIONS[-1][1]
    assert d == D_MODEL and seq % (BLK * max_dil) == 0
    assert seq % TOKEN_TILE == 0 and TOKEN_TILE % SUB_TILE == 0
    assert SUB_TILE % CHUNK == 0 and SUB_TILE % (16 * max_dil) == 0
    row = lambda a: a.reshape(1, -1).astype(F32)
    h = x.reshape(batch * seq, d)

    h, w_qkv = _layer0(h, row(mix_norm_g[0]), a_w_in[0].astype(BF16), row(a_ln_g[0]), row(a_ln_b[0]),
                       a_w_s[0], a_b_s[0].T, a_w_out[0].astype(BF16), row(mlp_norm_g[0]),
                       w_up[0].astype(BF16), w_down[0].astype(BF16), (b_w_qkv, 0))

    *qkv, w_up1, w_down1, w_out1 = _qkv(h, row(mix_norm_g[1]), w_qkv, [(w_up, 1), (w_down, 1), (b_w_out, 0)])
    tables = _bias_tables(rel_bias)
    os_, lses = [], []
    for g, (_, dil) in enumerate(WINDOW_DILATIONS):
        o, lse = _attention_group(qkv[g], qkv[N_DIL_GROUPS + g], qkv[2 * N_DIL_GROUPS + g],
                                  tables[g], batch, seq, dil, ATTN_BLOCKS_PER_STEP)
        os_.append(o)
        lses.append(lse)

    out = _layer1_tail(h, os_, lses, w_out1, row(mlp_norm_g[1]), w_up1, w_down1, row(final_norm_g))
    return out.reshape(batch, seq, d)
```

```python
import functools
import math

import jax
import jax.numpy as jnp
from jax import lax
from jax.experimental import pallas as pl
from jax.experimental.pallas import tpu as pltpu

D_MODEL = 1024
CHUNK = 128
GATE_WIDTH = D_MODEL
GATE_GROUPS = 8
WINDOW_DILATIONS = ((128, 1), (512, 4), (2048, 16))
N_DIL_GROUPS = len(WINDOW_DILATIONS)
ATT_HEADS = 8
HEAD_DIM = 64
ATT_WIDTH = ATT_HEADS * HEAD_DIM
N_BUCKETS = 32
MAX_EXACT = N_BUCKETS // 2
REL_MAX_DISTANCE = max(w for w, _ in WINDOW_DILATIONS)
D_FF = 4 * D_MODEL
EPS = 1e-6
NEG_INF = -1e30
LOG2E = math.log2(math.e)

BLK = 128
LANES = 128
HEADS_PER_VREG = LANES // HEAD_DIM
HEAD_PAIRS = ATT_HEADS // HEADS_PER_VREG
TOKEN_TILE = 1024
SUB_TILE = 512
LAYER0_SUB_TILE = 256
CHAIN_LAG = 2
FF_CHUNK = 1024
ATTN_BLOCKS_PER_STEP = 16
VMEM_LIMIT = 63 * 1024 * 1024

F32 = jnp.float32
BF16 = jnp.bfloat16


def _dot(a, b):
    return jnp.dot(a, b, preferred_element_type=F32)


def _dot_nt(a, b):
    return lax.dot_general(a, b, (((1,), (1,)), ((), ())), preferred_element_type=F32)


def _rms_norm(x, g):
    return x * lax.rsqrt(jnp.mean(x * x, axis=-1, keepdims=True) + EPS) * g


def _gelu_exact(x):
    return 0.5 * x * (1.0 + lax.erf(x * math.sqrt(0.5)))


def _interleave(chains, lag):
    chains = list(chains)
    alive = [True] * len(chains)
    t = 0
    while any(alive):
        for i, chain in enumerate(chains):
            if alive[i] and t >= i * lag:
                try:
                    next(chain)
                except StopIteration:
                    alive[i] = False
        t += 1


def _mlp_residual(h, g, wup_ref, wdn_ref):
    xn = _rms_norm(h, g).astype(BF16)
    acc = h
    for c in range(D_FF // FF_CHUNK):
        cols = slice(c * FF_CHUNK, (c + 1) * FF_CHUNK)
        up = _dot(xn, wup_ref[:, cols])
        yield
        act = jnp.square(jnp.maximum(up, 0.0)).astype(BF16)
        acc = acc + _dot(act, wdn_ref[cols, :])
        yield
    return acc


def _whole(shape):
    return pl.BlockSpec(shape, lambda *_: (0,) * len(shape))


def _cast_rider_specs(mats, n_steps):
    in_specs, out_specs, out_shape = [], [], []
    for arr, layer in mats:
        k, n = arr.shape[-2:]
        rows = k // n_steps
        assert k % n_steps == 0 and rows % 16 == 0
        in_specs.append(pl.BlockSpec((None, rows, n), lambda i, layer=layer: (layer, i, 0)))
        out_specs.append(pl.BlockSpec((rows, n), lambda i: (i, 0)))
        out_shape.append(jax.ShapeDtypeStruct((k, n), BF16))
    return in_specs, out_specs, out_shape


def _cast_riders(src_refs, dst_refs):
    for src, dst in zip(src_refs, dst_refs):
        dst[...] = src[...].astype(BF16)


def _layer0_chain(rows, causal, h_ref, mixg_ref, win_ref, lng_ref, lnb_ref, ws_ref, bst_ref,
                  wout_ref, mlpg_ref, wup_ref, wdn_ref, out_ref):
    nc = LAYER0_SUB_TILE // CHUNK
    h = h_ref[rows, :]
    xn = _rms_norm(h, mixg_ref[...]).astype(BF16)
    v = _gelu_exact(_dot(xn, win_ref[:, GATE_WIDTH:]))
    yield
    u = _gelu_exact(_dot(xn, win_ref[:, :GATE_WIDTH]))
    yield
    mu = jnp.mean(v, axis=-1, keepdims=True)
    vc = v - mu
    vn = vc * lax.rsqrt(jnp.mean(vc * vc, axis=-1, keepdims=True) + EPS)
    vb = (vn * lng_ref[...] + lnb_ref[...]).astype(BF16)
    mixed = []
    for g in range(GATE_GROUPS):
        wg = jnp.where(causal, ws_ref[g], 0.0).astype(BF16)
        lanes = slice(g * CHUNK, (g + 1) * CHUNK)
        vg = jnp.concatenate([vb[c * CHUNK:(c + 1) * CHUNK, lanes] for c in range(nc)], axis=1)
        mixed.append(_dot(wg, vg) + bst_ref[:, g:g + 1])
    gate = jnp.concatenate(
        [jnp.concatenate([mixed[g][:, c * CHUNK:(c + 1) * CHUNK] for g in range(GATE_GROUPS)], axis=1)
         for c in range(nc)], axis=0)
    yield
    h1 = h + _dot((u * gate).astype(BF16), wout_ref[...])
    yield
    out_ref[rows, :] = yield from _mlp_residual(h1, mlpg_ref[...], wup_ref, wdn_ref)


def _layer0_kernel(h_ref, *refs):
    *weights, cast_src, out_ref, cast_dst = refs
    t_idx = lax.broadcasted_iota(jnp.int32, (CHUNK, CHUNK), 0)
    s_idx = lax.broadcasted_iota(jnp.int32, (CHUNK, CHUNK), 1)
    causal = s_idx <= t_idx
    sub_tile = LAYER0_SUB_TILE
    _interleave((_layer0_chain(slice(sub * sub_tile, (sub + 1) * sub_tile), causal, h_ref, *weights, out_ref)
                 for sub in range(h_ref.shape[0] // sub_tile)), CHAIN_LAG)
    _cast_riders([cast_src], [cast_dst])


def _layer0(h, mix_g, w_in, ln_g, ln_b, w_s, b_st, w_out, mlp_g, w_up, w_down, to_cast):
    t = h.shape[0]
    n_steps = t // TOKEN_TILE
    tile = pl.BlockSpec((TOKEN_TILE, D_MODEL), lambda i: (i, 0))
    cast_in, cast_out, cast_shape = _cast_rider_specs([to_cast], n_steps)
    return pl.pallas_call(
        _layer0_kernel,
        grid=(n_steps,),
        in_specs=[tile, _whole((1, D_MODEL)), _whole(w_in.shape), _whole((1, GATE_WIDTH)),
                  _whole((1, GATE_WIDTH)), _whole(w_s.shape), _whole(b_st.shape), _whole(w_out.shape),
                  _whole((1, D_MODEL)), _whole(w_up.shape), _whole(w_down.shape)] + cast_in,
        out_specs=[tile] + cast_out,
        out_shape=[jax.ShapeDtypeStruct((t, D_MODEL), F32)] + cast_shape,
        compiler_params=pltpu.CompilerParams(dimension_semantics=("arbitrary",),
                                             vmem_limit_bytes=VMEM_LIMIT),
        name="layer0_gating_mlp",
    )(h, mix_g, w_in, ln_g, ln_b, w_s, b_st, w_out, mlp_g, w_up, w_down, to_cast[0])


def _qkv_chain(sub, h_ref, g_ref, w_ref, out_refs, xs_ref):
    n_col = D_MODEL // LANES
    xn = _rms_norm(h_ref[sub * SUB_TILE:(sub + 1) * SUB_TILE, :], g_ref[...])
    for c in range(n_col):
        xs_ref[sub, c] = xn[:, c * LANES:(c + 1) * LANES]
    for grp, (_, dil) in enumerate(WINDOW_DILATIONS):
        per = SUB_TILE // dil
        if dil == 1:
            x = xn
        else:
            x = jnp.concatenate(
                [jnp.concatenate([xs_ref[sub, c, pl.ds(r, per, stride=dil), :] for c in range(n_col)],
                                 axis=1) for r in range(dil)], axis=0)
        x = x.astype(BF16)
        for part in range(3):
            j = part * N_DIL_GROUPS + grp
            y = _dot(x, w_ref[:, j * ATT_WIDTH:(j + 1) * ATT_WIDTH])
            if part == 0:
                y = y * (HEAD_DIM ** -0.5 * LOG2E)
            y = y.astype(BF16)
            o_ref = out_refs[j]
            for r in range(dil):
                o_ref[sub * per:(sub + 1) * per, r * ATT_WIDTH:(r + 1) * ATT_WIDTH] = y[r * per:(r + 1) * per, :]
            yield


def _qkv_kernel(h_ref, g_ref, w_ref, *refs, n_cast):
    n_out = 3 * N_DIL_GROUPS
    cast_src, out_refs = refs[:n_cast], refs[n_cast:n_cast + n_out]
    cast_dst, xs_ref = refs[n_cast + n_out:-1], refs[-1]
    for sub in range(h_ref.shape[0] // SUB_TILE):
        for _ in _qkv_chain(sub, h_ref, g_ref, w_ref, out_refs, xs_ref):
            pass
    _cast_riders(cast_src, cast_dst)


def _qkv(h, g, w_qkv, to_cast):
    t = h.shape[0]
    n_steps = t // TOKEN_TILE
    tile = pl.BlockSpec((TOKEN_TILE, D_MODEL), lambda i: (i, 0))
    cast_in, cast_out, cast_shape = _cast_rider_specs(to_cast, n_steps)
    out_specs, out_shape = [], []
    for _ in range(3):
        for _, dil in WINDOW_DILATIONS:
            out_specs.append(pl.BlockSpec((TOKEN_TILE // dil, dil * ATT_WIDTH), lambda i: (i, 0)))
            out_shape.append(jax.ShapeDtypeStruct((t // dil, dil * ATT_WIDTH), BF16))
    return pl.pallas_call(
        functools.partial(_qkv_kernel, n_cast=len(to_cast)),
        grid=(n_steps,),
        in_specs=[tile, _whole((1, D_MODEL)), _whole(w_qkv.shape)] + cast_in,
        out_specs=out_specs + cast_out,
        out_shape=out_shape + cast_shape,
        scratch_shapes=[pltpu.VMEM((TOKEN_TILE // SUB_TILE, D_MODEL // LANES, SUB_TILE, LANES), F32)],
        compiler_params=pltpu.CompilerParams(dimension_semantics=("arbitrary",),
                                             vmem_limit_bytes=VMEM_LIMIT),
        name="qkv_proj",
    )(h, g, w_qkv, *[arr for arr, _ in to_cast])


def _t5_bucket(distance):
    small = distance < MAX_EXACT
    nf = jnp.maximum(distance, 1).astype(F32)
    large = MAX_EXACT + (jnp.log(nf / MAX_EXACT) / math.log(REL_MAX_DISTANCE / MAX_EXACT)
                         * (N_BUCKETS - MAX_EXACT)).astype(jnp.int32)
    large = jnp.minimum(large, N_BUCKETS - 1)
    return jnp.where(small, distance, large)


def _bias_table_kernel(rb_ref, bucket_ref, out_ref):
    g = pl.program_id(0)
    bucket = bucket_ref[...]
    i_idx = lax.broadcasted_iota(jnp.int32, (BLK, 2 * BLK), 0)
    j_idx = lax.broadcasted_iota(jnp.int32, (BLK, 2 * BLK), 1)
    rel = BLK + i_idx - j_idx
    band = (rel >= 0) & (rel <= BLK)
    for h in range(ATT_HEADS):
        acc = jnp.zeros((BLK, 2 * BLK), F32)
        for b in range(N_BUCKETS):
            acc = jnp.where(bucket == b, rb_ref[b, g * ATT_HEADS + h], acc)
        rows = slice((h % HEADS_PER_VREG) * BLK, (h % HEADS_PER_VREG + 1) * BLK)
        out_ref[0, h // HEADS_PER_VREG, rows, :] = jnp.where(band, acc * LOG2E, NEG_INF)
        out_ref[1, h // HEADS_PER_VREG, rows, :] = jnp.where(band & (j_idx >= BLK), acc * LOG2E, NEG_INF)


def _bias_tables(rel_bias):
    rel = BLK + jnp.arange(BLK)[:, None] - jnp.arange(2 * BLK)[None, :]
    buckets = jnp.stack([_t5_bucket(jnp.clip(rel, 0, BLK) * dil) for _, dil in WINDOW_DILATIONS])
    shape = (N_DIL_GROUPS, 2, HEAD_PAIRS, HEADS_PER_VREG * BLK, 2 * BLK)
    return pl.pallas_call(
        _bias_table_kernel,
        grid=(N_DIL_GROUPS,),
        in_specs=[pl.BlockSpec(memory_space=pltpu.SMEM),
                  pl.BlockSpec((None, BLK, 2 * BLK), lambda g: (g, 0, 0))],
        out_specs=pl.BlockSpec((None,) + shape[1:], lambda g: (g, 0, 0, 0, 0)),
        out_shape=jax.ShapeDtypeStruct(shape, F32),
        name="rel_bias_tables",
    )(rel_bias.astype(F32), buckets.astype(jnp.int32))


def _lse_lane(head):
    return (head % HEADS_PER_VREG) * HEAD_DIM + HEADS_PER_VREG * (head // HEADS_PER_VREG)


def _attn_kernel(*refs, whole_sequence):
    if whole_sequence:
        q_ref, k_ref, v_ref, tb_ref, o_ref, lse_ref = refs
    else:
        q_ref, k_ref, v_ref, kp_ref, vp_ref, tb_ref, o_ref, lse_ref = refs
    rows = q_ref.shape[0]

    def unit(q, keys, vals, variant):
        lane = lax.broadcasted_iota(jnp.int32, (BLK, LANES), 1)
        first = lane < HEAD_DIM
        ones = jnp.ones((2 * BLK, LANES), BF16)
        outs = []
        lse = jnp.zeros((BLK, LANES), F32)
        for pair in range(HEAD_PAIRS):
            sl = slice(pair * LANES, (pair + 1) * LANES)
            qp, kp, vp = q[:, sl], keys[:, sl], vals[:, sl]
            zero = jnp.zeros_like(qp)
            qq = jnp.concatenate([jnp.where(first, qp, zero), jnp.where(first, zero, qp)], axis=0)
            logits = _dot_nt(qq, kp) + tb_ref[variant, pair]
            m = jnp.max(logits, axis=-1, keepdims=True)
            p = jnp.exp2(logits - m).astype(BF16)
            res = _dot(p, jnp.concatenate([vp, ones], axis=1))
            num = jnp.where(first, res[:BLK, :LANES], res[BLK:, :LANES])
            den = jnp.where(first, res[:BLK, LANES:], res[BLK:, LANES:])
            outs.append((num / den).astype(BF16))
            lse_pair = jnp.where(first, m[:BLK], m[BLK:]) + jnp.log2(den)
            keep = (lane == _lse_lane(pair * HEADS_PER_VREG)) | (lane == _lse_lane(pair * HEADS_PER_VREG + 1))
            lse = jnp.where(keep, lse_pair, lse)
        return jnp.concatenate(outs, axis=1), lse

    first_variant = 1 if whole_sequence else jnp.where(pl.program_id(2) == 0, 1, 0)
    for s in range(q_ref.shape[1] // ATT_WIDTH):
        cols = slice(s * ATT_WIDTH, (s + 1) * ATT_WIDTH)
        lse_cols = slice(s * LANES, (s + 1) * LANES)
        for u in range(rows // BLK):
            cur_rows = slice(u * BLK, (u + 1) * BLK)
            if u > 0:
                both = slice((u - 1) * BLK, (u + 1) * BLK)
                o, lse = unit(q_ref[cur_rows, cols], k_ref[both, cols], v_ref[both, cols], 0)
            else:
                k_before = k_ref[cur_rows, cols] if whole_sequence else kp_ref[:, cols]
                v_before = v_ref[cur_rows, cols] if whole_sequence else vp_ref[:, cols]
                keys = jnp.concatenate([k_before, k_ref[cur_rows, cols]], axis=0)
                vals = jnp.concatenate([v_before, v_ref[cur_rows, cols]], axis=0)
                o, lse = unit(q_ref[cur_rows, cols], keys, vals, first_variant)
            o_ref[cur_rows, cols] = o
            lse_ref[cur_rows, lse_cols] = lse


def _attention_group(q, k, v, tables, group, batch, seq, dil, blocks):
    sub = seq // dil
    w = ATT_WIDTH
    rows = min(blocks * BLK, sub)
    n_seq = blocks * BLK // rows
    assert sub % rows == 0 and dil % n_seq == 0
    whole_sequence = rows == sub
    q, k, v = (a.reshape(batch, sub, dil * w) for a in (q, k, v))
    blocks_per_step = rows // BLK
    cur = pl.BlockSpec((None, rows, n_seq * w), lambda b, r, i: (b, i, r))
    prev = pl.BlockSpec((None, BLK, n_seq * w),
                        lambda b, r, i: (b, jnp.maximum(i * blocks_per_step - 1, 0), r))
    operands = (q, k, v) if whole_sequence else (q, k, v, k, v)
    in_specs = [cur, cur, cur] if whole_sequence else [cur, cur, cur, prev, prev]
    o, lse = pl.pallas_call(
        functools.partial(_attn_kernel, whole_sequence=whole_sequence),
        grid=(batch, dil // n_seq, sub // rows),
        in_specs=in_specs + [pl.BlockSpec((None,) + tables.shape[1:], lambda b, r, i: (group, 0, 0, 0, 0),
                                          pipeline_mode=pl.Buffered(1))],
        out_specs=[cur, pl.BlockSpec((None, rows, n_seq * LANES), lambda b, r, i: (b, i, r))],
        out_shape=[jax.ShapeDtypeStruct((batch, sub, dil * w), BF16),
                   jax.ShapeDtypeStruct((batch, sub, dil * LANES), F32)],
        compiler_params=pltpu.CompilerParams(
            dimension_semantics=("arbitrary", "arbitrary", "arbitrary"),
            vmem_limit_bytes=VMEM_LIMIT),
        name=f"dilated_attention_d{dil}",
    )(*operands, tables)
    return o.reshape(batch * sub, dil * w), lse.reshape(batch * sub, dil * LANES)


def _natural_order(ref, scr_ref, sub, dil, width):
    per = SUB_TILE // dil
    rows = slice(sub * per, (sub + 1) * per)
    if dil == 1:
        return ref[rows, :].astype(F32)
    n_col = width // LANES
    for r in range(dil):
        for c in range(n_col):
            lanes = slice(r * width + c * LANES, r * width + (c + 1) * LANES)
            scr_ref[c, pl.ds(r, per, stride=dil), :] = ref[rows, lanes].astype(F32)
    return jnp.concatenate([scr_ref[c] for c in range(n_col)], axis=1)


def _layer1_tail_chain(sub, expand, h_ref, o_refs, l_refs, wout_ref, mlpg_ref, wup_ref, wdn_ref,
                       fing_ref, out_ref, os_ref, ls_ref):
    rows = slice(sub * SUB_TILE, (sub + 1) * SUB_TILE)
    dils = [d for _, d in WINDOW_DILATIONS]
    lses = [_natural_order(l_ref, None if g == 0 else ls_ref.at[sub, g - 1], sub, dils[g], LANES)
            for g, l_ref in enumerate(l_refs)]
    outs = [_natural_order(o_ref, None if g == 0 else os_ref.at[sub, g - 1], sub, dils[g], ATT_WIDTH)
            for g, o_ref in enumerate(o_refs)]
    m_all = jnp.maximum(jnp.maximum(lses[0], lses[1]), lses[2])
    num = jnp.zeros(outs[0].shape, F32)
    den = jnp.zeros(outs[0].shape, F32)
    for l, og in zip(lses, outs):
        wide = _dot(jnp.exp2(l - m_all).astype(BF16), expand)
        num = num + wide * og
        den = den + wide
    yield
    h1 = h_ref[rows, :] + _dot((num / den).astype(BF16), wout_ref[...])
    yield
    h2 = yield from _mlp_residual(h1, mlpg_ref[...], wup_ref, wdn_ref)
    out_ref[rows, :] = _rms_norm(h2, fing_ref[...])


def _layer1_tail_kernel(h_ref, o0_ref, o1_ref, o2_ref, l0_ref, l1_ref, l2_ref, *refs):
    row = lax.broadcasted_iota(jnp.int32, (LANES, ATT_WIDTH), 0)
    head = lax.broadcasted_iota(jnp.int32, (LANES, ATT_WIDTH), 1) // HEAD_DIM
    src = (head % HEADS_PER_VREG) * HEAD_DIM + HEADS_PER_VREG * (head // HEADS_PER_VREG)
    expand = jnp.where(row == src, 1.0, 0.0).astype(BF16)
    _interleave((_layer1_tail_chain(sub, expand, h_ref, (o0_ref, o1_ref, o2_ref), (l0_ref, l1_ref, l2_ref), *refs)
                 for sub in range(h_ref.shape[0] // SUB_TILE)), CHAIN_LAG)


def _layer1_tail(h, os_, lses, w_out, mlp_g, w_up, w_down, fin_g):
    t = h.shape[0]
    tile = pl.BlockSpec((TOKEN_TILE, D_MODEL), lambda i: (i, 0))
    otiles = [pl.BlockSpec((TOKEN_TILE // d, d * ATT_WIDTH), lambda i: (i, 0)) for _, d in WINDOW_DILATIONS]
    ltiles = [pl.BlockSpec((TOKEN_TILE // d, d * LANES), lambda i: (i, 0)) for _, d in WINDOW_DILATIONS]
    return pl.pallas_call(
        _layer1_tail_kernel,
        grid=(t // TOKEN_TILE,),
        in_specs=[tile, *otiles, *ltiles, _whole(w_out.shape),
                  _whole((1, D_MODEL)), _whole(w_up.shape), _whole(w_down.shape),
                  _whole((1, D_MODEL))],
        out_specs=tile,
        out_shape=jax.ShapeDtypeStruct((t, D_MODEL), F32),
        scratch_shapes=[
            pltpu.VMEM((TOKEN_TILE // SUB_TILE, N_DIL_GROUPS - 1, ATT_WIDTH // LANES, SUB_TILE, LANES), F32),
            pltpu.VMEM((TOKEN_TILE // SUB_TILE, N_DIL_GROUPS - 1, 1, SUB_TILE, LANES), F32)],
        compiler_params=pltpu.CompilerParams(dimension_semantics=("arbitrary",),
                                             vmem_limit_bytes=VMEM_LIMIT),
        name="layer1_merge_proj_mlp",
    )(h, *os_, *lses, w_out, mlp_g, w_up, w_down, fin_g)


def kernel(x, mix_norm_g, mlp_norm_g, final_norm_g, a_w_in, a_ln_g, a_ln_b, a_w_s, a_b_s, a_w_out,
           b_w_qkv, b_w_out, rel_bias, w_up, w_down):
    batch, seq, d = x.shape
    max_dil = WINDOW_DILATIONS[-1][1]
    assert d == D_MODEL and seq % (BLK * max_dil) == 0
    assert seq % TOKEN_TILE == 0 and TOKEN_TILE % SUB_TILE == 0
    assert SUB_TILE % (16 * max_dil) == 0 and TOKEN_TILE % LAYER0_SUB_TILE == 0 and LAYER0_SUB_TILE % CHUNK == 0
    row = lambda a: a.reshape(1, -1).astype(F32)
    h = x.reshape(batch * seq, d)

    h, w_qkv = _layer0(h, row(mix_norm_g[0]), a_w_in[0].astype(BF16), row(a_ln_g[0]), row(a_ln_b[0]),
                       a_w_s[0], a_b_s[0].T, a_w_out[0].astype(BF16), row(mlp_norm_g[0]),
                       w_up[0].astype(BF16), w_down[0].astype(BF16), (b_w_qkv, 0))

    *qkv, w_up1, w_down1, w_out1 = _qkv(h, row(mix_norm_g[1]), w_qkv, [(w_up, 1), (w_down, 1), (b_w_out, 0)])
    tables = _bias_tables(rel_bias)
    os_, lses = [], []
    for g, (_, dil) in enumerate(WINDOW_DILATIONS):
        o, lse = _attention_group(qkv[g], qkv[N_DIL_GROUPS + g], qkv[2 * N_DIL_GROUPS + g],
                                  tables, g, batch, seq, dil, ATTN_BLOCKS_PER_STEP)
        os_.append(o)
        lses.append(lse)

    out = _layer1_tail(h, os_, lses, w_out1, row(mlp_norm_g[1]), w_up1, w_down1, row(final_norm_g))
    return out.reshape(batch, seq, d)
```

```python
import functools
import math

import jax
import jax.numpy as jnp
from jax import lax
from jax.experimental import pallas as pl
from jax.experimental.pallas import tpu as pltpu

D_MODEL = 1024
CHUNK = 128
GATE_WIDTH = D_MODEL
GATE_GROUPS = 8
WINDOW_DILATIONS = ((128, 1), (512, 4), (2048, 16))
N_DIL_GROUPS = len(WINDOW_DILATIONS)
ATT_HEADS = 8
HEAD_DIM = 64
ATT_WIDTH = ATT_HEADS * HEAD_DIM
N_BUCKETS = 32
MAX_EXACT = N_BUCKETS // 2
REL_MAX_DISTANCE = max(w for w, _ in WINDOW_DILATIONS)
D_FF = 4 * D_MODEL
EPS = 1e-6
NEG_INF = -1e30
LOG2E = math.log2(math.e)

BLK = 128
LANES = 128
HEADS_PER_VREG = LANES // HEAD_DIM
HEAD_PAIRS = ATT_HEADS // HEADS_PER_VREG
TOKEN_TILE = 1024
SUB_TILE = 512
LAYER0_SUB_TILE = 256
CHAIN_LAG = 2
FF_CHUNK = 1024
ATTN_BLOCKS_PER_STEP = 16
PREP_HEADS_PER_STEP = 3
VMEM_LIMIT = 63 * 1024 * 1024

F32 = jnp.float32
BF16 = jnp.bfloat16


def _dot(a, b):
    return jnp.dot(a, b, preferred_element_type=F32)


def _dot_nt(a, b):
    return lax.dot_general(a, b, (((1,), (1,)), ((), ())), preferred_element_type=F32)


def _rms_norm(x, g):
    return x * lax.rsqrt(jnp.mean(x * x, axis=-1, keepdims=True) + EPS) * g


def _gelu_exact(x):
    return 0.5 * x * (1.0 + lax.erf(x * math.sqrt(0.5)))


def _interleave(chains, lag):
    chains = list(chains)
    alive = [True] * len(chains)
    t = 0
    while any(alive):
        for i, chain in enumerate(chains):
            if alive[i] and t >= i * lag:
                try:
                    next(chain)
                except StopIteration:
                    alive[i] = False
        t += 1


def _mlp_residual(h, g, wup_ref, wdn_ref):
    xn = _rms_norm(h, g).astype(BF16)
    acc = h
    for c in range(D_FF // FF_CHUNK):
        cols = slice(c * FF_CHUNK, (c + 1) * FF_CHUNK)
        up = _dot(xn, wup_ref[:, cols])
        yield
        act = jnp.square(jnp.maximum(up, 0.0)).astype(BF16)
        acc = acc + _dot(act, wdn_ref[cols, :])
        yield
    return acc


def _whole(shape):
    return pl.BlockSpec(shape, lambda *_: (0,) * len(shape))


def _cast_rider_specs(mats, n_steps):
    in_specs, out_specs, out_shape = [], [], []
    for arr, layer in mats:
        k, n = arr.shape[-2:]
        rows = k // n_steps
        assert k % n_steps == 0 and rows % 16 == 0
        in_specs.append(pl.BlockSpec((None, rows, n), lambda i, layer=layer: (layer, i, 0)))
        out_specs.append(pl.BlockSpec((rows, n), lambda i: (i, 0)))
        out_shape.append(jax.ShapeDtypeStruct((k, n), BF16))
    return in_specs, out_specs, out_shape


def _cast_riders(src_refs, dst_refs):
    for src, dst in zip(src_refs, dst_refs):
        dst[...] = src[...].astype(BF16)


def _layer0_chain(rows, causal, h_ref, mixg_ref, win_ref, lng_ref, lnb_ref, ws_ref, bst_ref,
                  wout_ref, mlpg_ref, wup_ref, wdn_ref, out_ref):
    nc = LAYER0_SUB_TILE // CHUNK
    h = h_ref[rows, :]
    xn = _rms_norm(h, mixg_ref[...]).astype(BF16)
    v = _gelu_exact(_dot(xn, win_ref[:, GATE_WIDTH:]))
    yield
    u = _gelu_exact(_dot(xn, win_ref[:, :GATE_WIDTH]))
    yield
    mu = jnp.mean(v, axis=-1, keepdims=True)
    vc = v - mu
    vn = vc * lax.rsqrt(jnp.mean(vc * vc, axis=-1, keepdims=True) + EPS)
    vb = (vn * lng_ref[...] + lnb_ref[...]).astype(BF16)
    mixed = []
    for g in range(GATE_GROUPS):
        wg = jnp.where(causal, ws_ref[g], 0.0).astype(BF16)
        lanes = slice(g * CHUNK, (g + 1) * CHUNK)
        vg = jnp.concatenate([vb[c * CHUNK:(c + 1) * CHUNK, lanes] for c in range(nc)], axis=1)
        mixed.append(_dot(wg, vg) + bst_ref[:, g:g + 1])
    gate = jnp.concatenate(
        [jnp.concatenate([mixed[g][:, c * CHUNK:(c + 1) * CHUNK] for g in range(GATE_GROUPS)], axis=1)
         for c in range(nc)], axis=0)
    yield
    h1 = h + _dot((u * gate).astype(BF16), wout_ref[...])
    yield
    out_ref[rows, :] = yield from _mlp_residual(h1, mlpg_ref[...], wup_ref, wdn_ref)


def _layer0_kernel(h_ref, *refs):
    *weights, cast_src, out_ref, cast_dst = refs
    t_idx = lax.broadcasted_iota(jnp.int32, (CHUNK, CHUNK), 0)
    s_idx = lax.broadcasted_iota(jnp.int32, (CHUNK, CHUNK), 1)
    causal = s_idx <= t_idx
    sub_tile = LAYER0_SUB_TILE
    _interleave((_layer0_chain(slice(sub * sub_tile, (sub + 1) * sub_tile), causal, h_ref, *weights, out_ref)
                 for sub in range(h_ref.shape[0] // sub_tile)), CHAIN_LAG)
    _cast_riders([cast_src], [cast_dst])


def _layer0(h, mix_g, w_in, ln_g, ln_b, w_s, b_st, w_out, mlp_g, w_up, w_down, to_cast):
    t = h.shape[0]
    n_steps = t // TOKEN_TILE
    tile = pl.BlockSpec((TOKEN_TILE, D_MODEL), lambda i: (i, 0))
    cast_in, cast_out, cast_shape = _cast_rider_specs([to_cast], n_steps)
    return pl.pallas_call(
        _layer0_kernel,
        grid=(n_steps,),
        in_specs=[tile, _whole((1, D_MODEL)), _whole(w_in.shape), _whole((1, GATE_WIDTH)),
                  _whole((1, GATE_WIDTH)), _whole(w_s.shape), _whole(b_st.shape), _whole(w_out.shape),
                  _whole((1, D_MODEL)), _whole(w_up.shape), _whole(w_down.shape)] + cast_in,
        out_specs=[tile] + cast_out,
        out_shape=[jax.ShapeDtypeStruct((t, D_MODEL), F32)] + cast_shape,
        compiler_params=pltpu.CompilerParams(dimension_semantics=("arbitrary",),
                                             vmem_limit_bytes=VMEM_LIMIT),
        name="layer0_gating_mlp",
    )(h, mix_g, w_in, ln_g, ln_b, w_s, b_st, w_out, mlp_g, w_up, w_down, to_cast[0])


def _qkv_chain(sub, h_ref, g_ref, w_ref, out_refs, xs_ref):
    n_col = D_MODEL // LANES
    xn = _rms_norm(h_ref[sub * SUB_TILE:(sub + 1) * SUB_TILE, :], g_ref[...])
    for c in range(n_col):
        xs_ref[sub, c] = xn[:, c * LANES:(c + 1) * LANES]
    for grp, (_, dil) in enumerate(WINDOW_DILATIONS):
        per = SUB_TILE // dil
        if dil == 1:
            x = xn
        else:
            x = jnp.concatenate(
                [jnp.concatenate([xs_ref[sub, c, pl.ds(r, per, stride=dil), :] for c in range(n_col)],
                                 axis=1) for r in range(dil)], axis=0)
        x = x.astype(BF16)
        for part in range(3):
            j = part * N_DIL_GROUPS + grp
            y = _dot(x, w_ref[:, j * ATT_WIDTH:(j + 1) * ATT_WIDTH])
            if part == 0:
                y = y * (HEAD_DIM ** -0.5 * LOG2E)
            y = y.astype(BF16)
            o_ref = out_refs[j]
            for r in range(dil):
                o_ref[sub * per:(sub + 1) * per, r * ATT_WIDTH:(r + 1) * ATT_WIDTH] = y[r * per:(r + 1) * per, :]
            yield


def _qkv_kernel(h_ref, g_ref, w_ref, *refs, n_cast):
    n_out = 3 * N_DIL_GROUPS
    cast_src, out_refs = refs[:n_cast], refs[n_cast:n_cast + n_out]
    cast_dst, xs_ref = refs[n_cast + n_out:-1], refs[-1]
    for sub in range(h_ref.shape[0] // SUB_TILE):
        for _ in _qkv_chain(sub, h_ref, g_ref, w_ref, out_refs, xs_ref):
            pass
    _cast_riders(cast_src, cast_dst)


def _qkv(h, g, w_qkv, to_cast):
    t = h.shape[0]
    n_steps = t // TOKEN_TILE
    tile = pl.BlockSpec((TOKEN_TILE, D_MODEL), lambda i: (i, 0))
    cast_in, cast_out, cast_shape = _cast_rider_specs(to_cast, n_steps)
    out_specs, out_shape = [], []
    for _ in range(3):
        for _, dil in WINDOW_DILATIONS:
            out_specs.append(pl.BlockSpec((TOKEN_TILE // dil, dil * ATT_WIDTH), lambda i: (i, 0)))
            out_shape.append(jax.ShapeDtypeStruct((t // dil, dil * ATT_WIDTH), BF16))
    return pl.pallas_call(
        functools.partial(_qkv_kernel, n_cast=len(to_cast)),
        grid=(n_steps,),
        in_specs=[tile, _whole((1, D_MODEL)), _whole(w_qkv.shape)] + cast_in,
        out_specs=out_specs + cast_out,
        out_shape=out_shape + cast_shape,
        scratch_shapes=[pltpu.VMEM((TOKEN_TILE // SUB_TILE, D_MODEL // LANES, SUB_TILE, LANES), F32)],
        compiler_params=pltpu.CompilerParams(dimension_semantics=("arbitrary",),
                                             vmem_limit_bytes=VMEM_LIMIT),
        name="qkv_proj",
    )(h, g, w_qkv, *[arr for arr, _ in to_cast])


def _t5_bucket(distance):
    small = distance < MAX_EXACT
    nf = jnp.maximum(distance, 1).astype(F32)
    large = MAX_EXACT + (jnp.log(nf / MAX_EXACT) / math.log(REL_MAX_DISTANCE / MAX_EXACT)
                         * (N_BUCKETS - MAX_EXACT)).astype(jnp.int32)
    large = jnp.minimum(large, N_BUCKETS - 1)
    return jnp.where(small, distance, large)


def _prep_kernel(rb_ref, bucket_ref, *refs, n_cast):
    cast_src, tables_ref, cast_dst = refs[:n_cast], refs[n_cast], refs[n_cast + 1:]
    _cast_riders(cast_src, cast_dst)
    i_idx = lax.broadcasted_iota(jnp.int32, (BLK, 2 * BLK), 0)
    j_idx = lax.broadcasted_iota(jnp.int32, (BLK, 2 * BLK), 1)
    rel = BLK + i_idx - j_idx
    band = (rel >= 0) & (rel <= BLK)
    for k in range(PREP_HEADS_PER_STEP):
        gh = pl.program_id(0) * PREP_HEADS_PER_STEP + k
        g, h = gh // ATT_HEADS, gh % ATT_HEADS
        bucket = bucket_ref[g]
        acc = jnp.zeros((BLK, 2 * BLK), F32)
        for b in range(N_BUCKETS):
            acc = jnp.where(bucket == b, rb_ref[b, gh], acc)
        pair = h // HEADS_PER_VREG
        rows = pl.ds(pl.multiple_of((h % HEADS_PER_VREG) * BLK, BLK), BLK)
        tables_ref[g, 0, pair, rows, :] = jnp.where(band, acc * LOG2E, NEG_INF)
        tables_ref[g, 1, pair, rows, :] = jnp.where(band & (j_idx >= BLK), acc * LOG2E, NEG_INF)


def _prep(rel_bias, to_cast):
    n_steps = N_DIL_GROUPS * ATT_HEADS // PREP_HEADS_PER_STEP
    rel = BLK + jnp.arange(BLK)[:, None] - jnp.arange(2 * BLK)[None, :]
    buckets = jnp.stack([_t5_bucket(jnp.clip(rel, 0, BLK) * dil) for _, dil in WINDOW_DILATIONS])
    shape = (N_DIL_GROUPS, 2, HEAD_PAIRS, HEADS_PER_VREG * BLK, 2 * BLK)
    cast_in, cast_out, cast_shape = _cast_rider_specs(to_cast, n_steps)
    return pl.pallas_call(
        functools.partial(_prep_kernel, n_cast=len(to_cast)),
        grid=(n_steps,),
        in_specs=[pl.BlockSpec(memory_space=pltpu.SMEM), _whole(buckets.shape)] + cast_in,
        out_specs=[_whole(shape)] + cast_out,
        out_shape=[jax.ShapeDtypeStruct(shape, F32)] + cast_shape,
        compiler_params=pltpu.CompilerParams(dimension_semantics=("arbitrary",),
                                             vmem_limit_bytes=VMEM_LIMIT),
        name="bias_tables_and_casts",
    )(rel_bias.astype(F32), buckets.astype(jnp.int32), *[arr for arr, _ in to_cast])


def _lse_lane(head):
    return (head % HEADS_PER_VREG) * HEAD_DIM + HEADS_PER_VREG * (head // HEADS_PER_VREG)


def _attn_kernel(*refs, whole_sequence):
    if whole_sequence:
        q_ref, k_ref, v_ref, tb_ref, o_ref, lse_ref = refs
    else:
        q_ref, k_ref, v_ref, kp_ref, vp_ref, tb_ref, o_ref, lse_ref = refs
    rows = q_ref.shape[0]

    def unit(q, keys, vals, variant):
        lane = lax.broadcasted_iota(jnp.int32, (BLK, LANES), 1)
        first = lane < HEAD_DIM
        ones = jnp.ones((2 * BLK, LANES), BF16)
        outs = []
        lse = jnp.zeros((BLK, LANES), F32)
        for pair in range(HEAD_PAIRS):
            sl = slice(pair * LANES, (pair + 1) * LANES)
            qp, kp, vp = q[:, sl], keys[:, sl], vals[:, sl]
            zero = jnp.zeros_like(qp)
            qq = jnp.concatenate([jnp.where(first, qp, zero), jnp.where(first, zero, qp)], axis=0)
            logits = _dot_nt(qq, kp) + tb_ref[variant, pair]
            m = jnp.max(logits, axis=-1, keepdims=True)
            p = jnp.exp2(logits - m).astype(BF16)
            res = _dot(p, jnp.concatenate([vp, ones], axis=1))
            num = jnp.where(first, res[:BLK, :LANES], res[BLK:, :LANES])
            den = jnp.where(first, res[:BLK, LANES:], res[BLK:, LANES:])
            outs.append((num / den).astype(BF16))
            lse_pair = jnp.where(first, m[:BLK], m[BLK:]) + jnp.log2(den)
            keep = (lane == _lse_lane(pair * HEADS_PER_VREG)) | (lane == _lse_lane(pair * HEADS_PER_VREG + 1))
            lse = jnp.where(keep, lse_pair, lse)
        return jnp.concatenate(outs, axis=1), lse

    first_variant = 1 if whole_sequence else jnp.where(pl.program_id(2) == 0, 1, 0)
    for s in range(q_ref.shape[1] // ATT_WIDTH):
        cols = slice(s * ATT_WIDTH, (s + 1) * ATT_WIDTH)
        lse_cols = slice(s * LANES, (s + 1) * LANES)
        for u in range(rows // BLK):
            cur_rows = slice(u * BLK, (u + 1) * BLK)
            if u > 0:
                both = slice((u - 1) * BLK, (u + 1) * BLK)
                o, lse = unit(q_ref[cur_rows, cols], k_ref[both, cols], v_ref[both, cols], 0)
            else:
                k_before = k_ref[cur_rows, cols] if whole_sequence else kp_ref[:, cols]
                v_before = v_ref[cur_rows, cols] if whole_sequence else vp_ref[:, cols]
                keys = jnp.concatenate([k_before, k_ref[cur_rows, cols]], axis=0)
                vals = jnp.concatenate([v_before, v_ref[cur_rows, cols]], axis=0)
                o, lse = unit(q_ref[cur_rows, cols], keys, vals, first_variant)
            o_ref[cur_rows, cols] = o
            lse_ref[cur_rows, lse_cols] = lse


def _attention_group(q, k, v, tables, group, batch, seq, dil, blocks):
    sub = seq // dil
    w = ATT_WIDTH
    rows = min(blocks * BLK, sub)
    n_seq = blocks * BLK // rows
    assert sub % rows == 0 and dil % n_seq == 0
    whole_sequence = rows == sub
    q, k, v = (a.reshape(batch, sub, dil * w) for a in (q, k, v))
    blocks_per_step = rows // BLK
    cur = pl.BlockSpec((None, rows, n_seq * w), lambda b, r, i: (b, i, r))
    prev = pl.BlockSpec((None, BLK, n_seq * w),
                        lambda b, r, i: (b, jnp.maximum(i * blocks_per_step - 1, 0), r))
    operands = (q, k, v) if whole_sequence else (q, k, v, k, v)
    in_specs = [cur, cur, cur] if whole_sequence else [cur, cur, cur, prev, prev]
    o, lse = pl.pallas_call(
        functools.partial(_attn_kernel, whole_sequence=whole_sequence),
        grid=(batch, dil // n_seq, sub // rows),
        in_specs=in_specs + [pl.BlockSpec((None,) + tables.shape[1:], lambda b, r, i: (group, 0, 0, 0, 0),
                                          pipeline_mode=pl.Buffered(1))],
        out_specs=[cur, pl.BlockSpec((None, rows, n_seq * LANES), lambda b, r, i: (b, i, r))],
        out_shape=[jax.ShapeDtypeStruct((batch, sub, dil * w), BF16),
                   jax.ShapeDtypeStruct((batch, sub, dil * LANES), F32)],
        compiler_params=pltpu.CompilerParams(
            dimension_semantics=("arbitrary", "arbitrary", "arbitrary"),
            vmem_limit_bytes=VMEM_LIMIT),
        name=f"dilated_attention_d{dil}",
    )(*operands, tables)
    return o.reshape(batch * sub, dil * w), lse.reshape(batch * sub, dil * LANES)


def _natural_order(ref, scr_ref, sub, dil, width):
    per = SUB_TILE // dil
    rows = slice(sub * per, (sub + 1) * per)
    if dil == 1:
        return ref[rows, :].astype(F32)
    n_col = width // LANES
    for r in range(dil):
        for c in range(n_col):
            lanes = slice(r * width + c * LANES, r * width + (c + 1) * LANES)
            scr_ref[c, pl.ds(r, per, stride=dil), :] = ref[rows, lanes].astype(F32)
    return jnp.concatenate([scr_ref[c] for c in range(n_col)], axis=1)


def _layer1_tail_chain(sub, expand, h_ref, o_refs, l_refs, wout_ref, mlpg_ref, wup_ref, wdn_ref,
                       fing_ref, out_ref, os_ref, ls_ref):
    rows = slice(sub * SUB_TILE, (sub + 1) * SUB_TILE)
    dils = [d for _, d in WINDOW_DILATIONS]
    lses = [_natural_order(l_ref, None if g == 0 else ls_ref.at[sub, g - 1], sub, dils[g], LANES)
            for g, l_ref in enumerate(l_refs)]
    outs = [_natural_order(o_ref, None if g == 0 else os_ref.at[sub, g - 1], sub, dils[g], ATT_WIDTH)
            for g, o_ref in enumerate(o_refs)]
    m_all = jnp.maximum(jnp.maximum(lses[0], lses[1]), lses[2])
    num = jnp.zeros(outs[0].shape, F32)
    den = jnp.zeros(outs[0].shape, F32)
    for l, og in zip(lses, outs):
        wide = _dot(jnp.exp2(l - m_all).astype(BF16), expand)
        num = num + wide * og
        den = den + wide
    yield
    h1 = h_ref[rows, :] + _dot((num / den).astype(BF16), wout_ref[...])
    yield
    h2 = yield from _mlp_residual(h1, mlpg_ref[...], wup_ref, wdn_ref)
    out_ref[rows, :] = _rms_norm(h2, fing_ref[...])


def _layer1_tail_kernel(h_ref, o0_ref, o1_ref, o2_ref, l0_ref, l1_ref, l2_ref, *refs):
    row = lax.broadcasted_iota(jnp.int32, (LANES, ATT_WIDTH), 0)
    head = lax.broadcasted_iota(jnp.int32, (LANES, ATT_WIDTH), 1) // HEAD_DIM
    src = (head % HEADS_PER_VREG) * HEAD_DIM + HEADS_PER_VREG * (head // HEADS_PER_VREG)
    expand = jnp.where(row == src, 1.0, 0.0).astype(BF16)
    _interleave((_layer1_tail_chain(sub, expand, h_ref, (o0_ref, o1_ref, o2_ref), (l0_ref, l1_ref, l2_ref), *refs)
                 for sub in range(h_ref.shape[0] // SUB_TILE)), CHAIN_LAG)


def _layer1_tail(h, os_, lses, w_out, mlp_g, w_up, w_down, fin_g):
    t = h.shape[0]
    tile = pl.BlockSpec((TOKEN_TILE, D_MODEL), lambda i: (i, 0))
    otiles = [pl.BlockSpec((TOKEN_TILE // d, d * ATT_WIDTH), lambda i: (i, 0)) for _, d in WINDOW_DILATIONS]
    ltiles = [pl.BlockSpec((TOKEN_TILE // d, d * LANES), lambda i: (i, 0)) for _, d in WINDOW_DILATIONS]
    return pl.pallas_call(
        _layer1_tail_kernel,
        grid=(t // TOKEN_TILE,),
        in_specs=[tile, *otiles, *ltiles, _whole(w_out.shape),
                  _whole((1, D_MODEL)), _whole(w_up.shape), _whole(w_down.shape),
                  _whole((1, D_MODEL))],
        out_specs=tile,
        out_shape=jax.ShapeDtypeStruct((t, D_MODEL), F32),
        scratch_shapes=[
            pltpu.VMEM((TOKEN_TILE // SUB_TILE, N_DIL_GROUPS - 1, ATT_WIDTH // LANES, SUB_TILE, LANES), F32),
            pltpu.VMEM((TOKEN_TILE // SUB_TILE, N_DIL_GROUPS - 1, 1, SUB_TILE, LANES), F32)],
        compiler_params=pltpu.CompilerParams(dimension_semantics=("arbitrary",),
                                             vmem_limit_bytes=VMEM_LIMIT),
        name="layer1_merge_proj_mlp",
    )(h, *os_, *lses, w_out, mlp_g, w_up, w_down, fin_g)


def kernel(x, mix_norm_g, mlp_norm_g, final_norm_g, a_w_in, a_ln_g, a_ln_b, a_w_s, a_b_s, a_w_out,
           b_w_qkv, b_w_out, rel_bias, w_up, w_down):
    batch, seq, d = x.shape
    max_dil = WINDOW_DILATIONS[-1][1]
    assert d == D_MODEL and seq % (BLK * max_dil) == 0
    assert seq % TOKEN_TILE == 0 and TOKEN_TILE % SUB_TILE == 0
    assert SUB_TILE % (16 * max_dil) == 0 and TOKEN_TILE % LAYER0_SUB_TILE == 0 and LAYER0_SUB_TILE % CHUNK == 0
    row = lambda a: a.reshape(1, -1).astype(F32)
    h = x.reshape(batch * seq, d)

    tables, w_in0, w_out0, w_up0, w_down0 = _prep(rel_bias, [(a_w_in, 0), (a_w_out, 0), (w_up, 0), (w_down, 0)])
    h, w_qkv = _layer0(h, row(mix_norm_g[0]), w_in0, row(a_ln_g[0]), row(a_ln_b[0]), a_w_s[0], a_b_s[0].T,
                       w_out0, row(mlp_norm_g[0]), w_up0, w_down0, (b_w_qkv, 0))

    *qkv, w_up1, w_down1, w_out1 = _qkv(h, row(mix_norm_g[1]), w_qkv, [(w_up, 1), (w_down, 1), (b_w_out, 0)])
    os_, lses = [], []
    for g, (_, dil) in enumerate(WINDOW_DILATIONS):
        o, lse = _attention_group(qkv[g], qkv[N_DIL_GROUPS + g], qkv[2 * N_DIL_GROUPS + g],
                                  tables, g, batch, seq, dil, ATTN_BLOCKS_PER_STEP)
        os_.append(o)
        lses.append(lse)

    out = _layer1_tail(h, os_, lses, w_out1, row(mlp_norm_g[1]), w_up1, w_down1, row(final_norm_g))
    return out.reshape(batch, seq, d)
```

```python
import functools
import math

import jax
import jax.numpy as jnp
from jax import lax
from jax.experimental import pallas as pl
from jax.experimental.pallas import tpu as pltpu

D_MODEL = 1024
CHUNK = 128
GATE_WIDTH = D_MODEL
GATE_GROUPS = 8
WINDOW_DILATIONS = ((128, 1), (512, 4), (2048, 16))
N_DIL_GROUPS = len(WINDOW_DILATIONS)
ATT_HEADS = 8
HEAD_DIM = 64
ATT_WIDTH = ATT_HEADS * HEAD_DIM
N_BUCKETS = 32
MAX_EXACT = N_BUCKETS // 2
REL_MAX_DISTANCE = max(w for w, _ in WINDOW_DILATIONS)
D_FF = 4 * D_MODEL
EPS = 1e-6
NEG_INF = -1e30
LOG2E = math.log2(math.e)

BLK = 128
LANES = 128
HEADS_PER_VREG = LANES // HEAD_DIM
HEAD_PAIRS = ATT_HEADS // HEADS_PER_VREG
TOKEN_TILE = 1024
SUB_TILE = 512
LAYER0_SUB_TILE = 256
CHAIN_LAG = 2
FF_CHUNK = 1024
ATTN_BLOCKS_PER_STEP = 16
PREP_HEADS_PER_STEP = 3
VMEM_LIMIT = 63 * 1024 * 1024

F32 = jnp.float32
BF16 = jnp.bfloat16


def _dot(a, b):
    return jnp.dot(a, b, preferred_element_type=F32)


def _dot_nt(a, b):
    return lax.dot_general(a, b, (((1,), (1,)), ((), ())), preferred_element_type=F32)


def _rms_norm(x, g):
    return x * lax.rsqrt(jnp.mean(x * x, axis=-1, keepdims=True) + EPS) * g


def _gelu_exact(x):
    return 0.5 * x * (1.0 + lax.erf(x * math.sqrt(0.5)))


def _interleave(chains, lag):
    chains = list(chains)
    alive = [True] * len(chains)
    t = 0
    while any(alive):
        for i, chain in enumerate(chains):
            if alive[i] and t >= i * lag:
                try:
                    next(chain)
                except StopIteration:
                    alive[i] = False
        t += 1


def _mlp_residual(h, g, wup_ref, wdn_ref):
    xn = _rms_norm(h, g).astype(BF16)
    acc = h
    for c in range(D_FF // FF_CHUNK):
        cols = slice(c * FF_CHUNK, (c + 1) * FF_CHUNK)
        up = _dot(xn, wup_ref[:, cols])
        yield
        act = jnp.square(jnp.maximum(up, 0.0)).astype(BF16)
        acc = acc + _dot(act, wdn_ref[cols, :])
        yield
    return acc


def _whole(shape):
    return pl.BlockSpec(shape, lambda *_: (0,) * len(shape))


def _cast_rider_specs(mats, n_steps):
    in_specs, out_specs, out_shape = [], [], []
    for arr, layer in mats:
        k, n = arr.shape[-2:]
        rows = k // n_steps
        assert k % n_steps == 0 and rows % 16 == 0
        in_specs.append(pl.BlockSpec((None, rows, n), lambda i, layer=layer: (layer, i, 0)))
        out_specs.append(pl.BlockSpec((rows, n), lambda i: (i, 0)))
        out_shape.append(jax.ShapeDtypeStruct((k, n), BF16))
    return in_specs, out_specs, out_shape


def _cast_riders(src_refs, dst_refs):
    for src, dst in zip(src_refs, dst_refs):
        dst[...] = src[...].astype(BF16)


def _layer0_chain(rows, causal, h_ref, mixg_ref, win_ref, lng_ref, lnb_ref, ws_ref, bst_ref,
                  wout_ref, mlpg_ref, wup_ref, wdn_ref, out_ref):
    nc = LAYER0_SUB_TILE // CHUNK
    h = h_ref[rows, :]
    xn = _rms_norm(h, mixg_ref[...]).astype(BF16)
    v = _gelu_exact(_dot(xn, win_ref[:, GATE_WIDTH:]))
    yield
    u = _gelu_exact(_dot(xn, win_ref[:, :GATE_WIDTH]))
    yield
    mu = jnp.mean(v, axis=-1, keepdims=True)
    vc = v - mu
    vn = vc * lax.rsqrt(jnp.mean(vc * vc, axis=-1, keepdims=True) + EPS)
    vb = (vn * lng_ref[...] + lnb_ref[...]).astype(BF16)
    mixed = []
    for g in range(GATE_GROUPS):
        wg = jnp.where(causal, ws_ref[g], 0.0).astype(BF16)
        lanes = slice(g * CHUNK, (g + 1) * CHUNK)
        vg = jnp.concatenate([vb[c * CHUNK:(c + 1) * CHUNK, lanes] for c in range(nc)], axis=1)
        mixed.append(_dot(wg, vg) + bst_ref[:, g:g + 1])
    gate = jnp.concatenate(
        [jnp.concatenate([mixed[g][:, c * CHUNK:(c + 1) * CHUNK] for g in range(GATE_GROUPS)], axis=1)
         for c in range(nc)], axis=0)
    yield
    h1 = h + _dot((u * gate).astype(BF16), wout_ref[...])
    yield
    out_ref[rows, :] = yield from _mlp_residual(h1, mlpg_ref[...], wup_ref, wdn_ref)


def _layer0_kernel(h_ref, *refs):
    *weights, cast_src, out_ref, cast_dst = refs
    t_idx = lax.broadcasted_iota(jnp.int32, (CHUNK, CHUNK), 0)
    s_idx = lax.broadcasted_iota(jnp.int32, (CHUNK, CHUNK), 1)
    causal = s_idx <= t_idx
    sub_tile = LAYER0_SUB_TILE
    _interleave((_layer0_chain(slice(sub * sub_tile, (sub + 1) * sub_tile), causal, h_ref, *weights, out_ref)
                 for sub in range(h_ref.shape[0] // sub_tile)), CHAIN_LAG)
    _cast_riders([cast_src], [cast_dst])


def _layer0(h, mix_g, w_in, ln_g, ln_b, w_s, b_st, w_out, mlp_g, w_up, w_down, to_cast):
    t = h.shape[0]
    n_steps = t // TOKEN_TILE
    tile = pl.BlockSpec((TOKEN_TILE, D_MODEL), lambda i: (i, 0))
    cast_in, cast_out, cast_shape = _cast_rider_specs([to_cast], n_steps)
    return pl.pallas_call(
        _layer0_kernel,
        grid=(n_steps,),
        in_specs=[tile, _whole((1, D_MODEL)), _whole(w_in.shape), _whole((1, GATE_WIDTH)),
                  _whole((1, GATE_WIDTH)), _whole(w_s.shape), _whole(b_st.shape), _whole(w_out.shape),
                  _whole((1, D_MODEL)), _whole(w_up.shape), _whole(w_down.shape)] + cast_in,
        out_specs=[tile] + cast_out,
        out_shape=[jax.ShapeDtypeStruct((t, D_MODEL), F32)] + cast_shape,
        compiler_params=pltpu.CompilerParams(dimension_semantics=("arbitrary",),
                                             vmem_limit_bytes=VMEM_LIMIT),
        name="layer0_gating_mlp",
    )(h, mix_g, w_in, ln_g, ln_b, w_s, b_st, w_out, mlp_g, w_up, w_down, to_cast[0])


def _qkv_chain(sub, h_ref, g_ref, w_ref, out_refs, xs_ref):
    n_col = D_MODEL // LANES
    xn = _rms_norm(h_ref[sub * SUB_TILE:(sub + 1) * SUB_TILE, :], g_ref[...])
    for c in range(n_col):
        xs_ref[sub, c] = xn[:, c * LANES:(c + 1) * LANES]
    for grp, (_, dil) in enumerate(WINDOW_DILATIONS):
        per = SUB_TILE // dil
        if dil == 1:
            x = xn
        else:
            x = jnp.concatenate(
                [jnp.concatenate([xs_ref[sub, c, pl.ds(r, per, stride=dil), :] for c in range(n_col)],
                                 axis=1) for r in range(dil)], axis=0)
        x = x.astype(BF16)
        for part in range(3):
            j = part * N_DIL_GROUPS + grp
            y = _dot(x, w_ref[:, j * ATT_WIDTH:(j + 1) * ATT_WIDTH])
            if part == 0:
                y = y * (HEAD_DIM ** -0.5 * LOG2E)
            y = y.astype(BF16)
            o_ref = out_refs[j]
            for r in range(dil):
                o_ref[sub * per:(sub + 1) * per, r * ATT_WIDTH:(r + 1) * ATT_WIDTH] = y[r * per:(r + 1) * per, :]
            yield


def _qkv_kernel(h_ref, g_ref, w_ref, *refs, n_cast):
    n_out = 3 * N_DIL_GROUPS
    cast_src, out_refs = refs[:n_cast], refs[n_cast:n_cast + n_out]
    cast_dst, xs_ref = refs[n_cast + n_out:-1], refs[-1]
    for sub in range(h_ref.shape[0] // SUB_TILE):
        for _ in _qkv_chain(sub, h_ref, g_ref, w_ref, out_refs, xs_ref):
            pass
    _cast_riders(cast_src, cast_dst)


def _qkv(h, g, w_qkv, to_cast):
    t = h.shape[0]
    n_steps = t // TOKEN_TILE
    tile = pl.BlockSpec((TOKEN_TILE, D_MODEL), lambda i: (i, 0))
    cast_in, cast_out, cast_shape = _cast_rider_specs(to_cast, n_steps)
    out_specs, out_shape = [], []
    for _ in range(3):
        for _, dil in WINDOW_DILATIONS:
            out_specs.append(pl.BlockSpec((TOKEN_TILE // dil, dil * ATT_WIDTH), lambda i: (i, 0)))
            out_shape.append(jax.ShapeDtypeStruct((t // dil, dil * ATT_WIDTH), BF16))
    return pl.pallas_call(
        functools.partial(_qkv_kernel, n_cast=len(to_cast)),
        grid=(n_steps,),
        in_specs=[tile, _whole((1, D_MODEL)), _whole(w_qkv.shape)] + cast_in,
        out_specs=out_specs + cast_out,
        out_shape=out_shape + cast_shape,
        scratch_shapes=[pltpu.VMEM((TOKEN_TILE // SUB_TILE, D_MODEL // LANES, SUB_TILE, LANES), F32)],
        compiler_params=pltpu.CompilerParams(dimension_semantics=("arbitrary",),
                                             vmem_limit_bytes=VMEM_LIMIT),
        name="qkv_proj",
    )(h, g, w_qkv, *[arr for arr, _ in to_cast])


def _t5_bucket(distance):
    small = distance < MAX_EXACT
    nf = jnp.maximum(distance, 1).astype(F32)
    large = MAX_EXACT + (jnp.log(nf / MAX_EXACT) / math.log(REL_MAX_DISTANCE / MAX_EXACT)
                         * (N_BUCKETS - MAX_EXACT)).astype(jnp.int32)
    large = jnp.minimum(large, N_BUCKETS - 1)
    return jnp.where(small, distance, large)


def _prep_kernel(rb_ref, bucket_ref, *refs, n_cast):
    cast_src, tables_ref, cast_dst = refs[:n_cast], refs[n_cast], refs[n_cast + 1:]
    _cast_riders(cast_src, cast_dst)
    i_idx = lax.broadcasted_iota(jnp.int32, (BLK, 2 * BLK), 0)
    j_idx = lax.broadcasted_iota(jnp.int32, (BLK, 2 * BLK), 1)
    rel = BLK + i_idx - j_idx
    band = (rel >= 0) & (rel <= BLK)
    for k in range(PREP_HEADS_PER_STEP):
        gh = pl.program_id(0) * PREP_HEADS_PER_STEP + k
        g, h = gh // ATT_HEADS, gh % ATT_HEADS
        bucket = bucket_ref[g]
        acc = jnp.zeros((BLK, 2 * BLK), F32)
        for b in range(N_BUCKETS):
            acc = jnp.where(bucket == b, rb_ref[b, gh], acc)
        pair = h // HEADS_PER_VREG
        rows = pl.ds(pl.multiple_of((h % HEADS_PER_VREG) * BLK, BLK), BLK)
        tables_ref[g, 0, pair, rows, :] = jnp.where(band, acc * LOG2E, NEG_INF)
        tables_ref[g, 1, pair, rows, :] = jnp.where(band & (j_idx >= BLK), acc * LOG2E, NEG_INF)


def _prep(rel_bias, to_cast):
    n_steps = N_DIL_GROUPS * ATT_HEADS // PREP_HEADS_PER_STEP
    rel = BLK + jnp.arange(BLK)[:, None] - jnp.arange(2 * BLK)[None, :]
    buckets = jnp.stack([_t5_bucket(jnp.clip(rel, 0, BLK) * dil) for _, dil in WINDOW_DILATIONS])
    shape = (N_DIL_GROUPS, 2, HEAD_PAIRS, HEADS_PER_VREG * BLK, 2 * BLK)
    cast_in, cast_out, cast_shape = _cast_rider_specs(to_cast, n_steps)
    return pl.pallas_call(
        functools.partial(_prep_kernel, n_cast=len(to_cast)),
        grid=(n_steps,),
        in_specs=[pl.BlockSpec(memory_space=pltpu.SMEM), _whole(buckets.shape)] + cast_in,
        out_specs=[_whole(shape)] + cast_out,
        out_shape=[jax.ShapeDtypeStruct(shape, F32)] + cast_shape,
        compiler_params=pltpu.CompilerParams(dimension_semantics=("arbitrary",),
                                             vmem_limit_bytes=VMEM_LIMIT),
        name="bias_tables_and_casts",
    )(rel_bias.astype(F32), buckets.astype(jnp.int32), *[arr for arr, _ in to_cast])


def _lse_lane(head):
    return (head % HEADS_PER_VREG) * HEAD_DIM + HEADS_PER_VREG * (head // HEADS_PER_VREG)


def _attn_kernel(*refs, whole_sequence):
    if whole_sequence:
        q_ref, k_ref, v_ref, tb_ref, o_ref, lse_ref = refs
    else:
        q_ref, k_ref, v_ref, kp_ref, vp_ref, tb_ref, o_ref, lse_ref = refs
    rows = q_ref.shape[0]

    def unit(q, keys, vals, variant):
        lane = lax.broadcasted_iota(jnp.int32, (BLK, LANES), 1)
        first = lane < HEAD_DIM
        ones = jnp.ones((2 * BLK, LANES), BF16)
        outs = []
        lse = jnp.zeros((BLK, LANES), F32)
        for pair in range(HEAD_PAIRS):
            sl = slice(pair * LANES, (pair + 1) * LANES)
            qp, kp, vp = q[:, sl], keys[:, sl], vals[:, sl]
            zero = jnp.zeros_like(qp)
            qq = jnp.concatenate([jnp.where(first, qp, zero), jnp.where(first, zero, qp)], axis=0)
            logits = _dot_nt(qq, kp) + tb_ref[variant, pair]
            m = jnp.max(logits, axis=-1, keepdims=True)
            p = jnp.exp2(logits - m).astype(BF16)
            res = _dot(p, jnp.concatenate([vp, ones], axis=1))
            num = jnp.where(first, res[:BLK, :LANES], res[BLK:, :LANES])
            den = jnp.where(first, res[:BLK, LANES:], res[BLK:, LANES:])
            outs.append((num / den).astype(BF16))
            lse_pair = jnp.where(first, m[:BLK], m[BLK:]) + jnp.log2(den)
            keep = (lane == _lse_lane(pair * HEADS_PER_VREG)) | (lane == _lse_lane(pair * HEADS_PER_VREG + 1))
            lse = jnp.where(keep, lse_pair, lse)
        return jnp.concatenate(outs, axis=1), lse

    first_variant = 1 if whole_sequence else jnp.where(pl.program_id(2) == 0, 1, 0)
    for s in range(q_ref.shape[1] // ATT_WIDTH):
        cols = slice(s * ATT_WIDTH, (s + 1) * ATT_WIDTH)
        lse_cols = slice(s * LANES, (s + 1) * LANES)
        for u in range(rows // BLK):
            cur_rows = slice(u * BLK, (u + 1) * BLK)
            if u > 0:
                both = slice((u - 1) * BLK, (u + 1) * BLK)
                o, lse = unit(q_ref[cur_rows, cols], k_ref[both, cols], v_ref[both, cols], 0)
            else:
                k_before = k_ref[cur_rows, cols] if whole_sequence else kp_ref[:, cols]
                v_before = v_ref[cur_rows, cols] if whole_sequence else vp_ref[:, cols]
                keys = jnp.concatenate([k_before, k_ref[cur_rows, cols]], axis=0)
                vals = jnp.concatenate([v_before, v_ref[cur_rows, cols]], axis=0)
                o, lse = unit(q_ref[cur_rows, cols], keys, vals, first_variant)
            o_ref[cur_rows, cols] = o
            lse_ref[cur_rows, lse_cols] = lse


def _attention_group(q, k, v, tables, group, batch, seq, dil, blocks):
    sub = seq // dil
    w = ATT_WIDTH
    rows = min(blocks * BLK, sub)
    n_seq = blocks * BLK // rows
    assert sub % rows == 0 and dil % n_seq == 0
    whole_sequence = rows == sub
    q, k, v = (a.reshape(batch, sub, dil * w) for a in (q, k, v))
    blocks_per_step = rows // BLK
    cur = pl.BlockSpec((None, rows, n_seq * w), lambda b, r, i: (b, i, r))
    prev = pl.BlockSpec((None, BLK, n_seq * w),
                        lambda b, r, i: (b, jnp.maximum(i * blocks_per_step - 1, 0), r))
    operands = (q, k, v) if whole_sequence else (q, k, v, k, v)
    in_specs = [cur, cur, cur] if whole_sequence else [cur, cur, cur, prev, prev]
    o, lse = pl.pallas_call(
        functools.partial(_attn_kernel, whole_sequence=whole_sequence),
        grid=(batch, dil // n_seq, sub // rows),
        in_specs=in_specs + [pl.BlockSpec((None,) + tables.shape[1:], lambda b, r, i: (group, 0, 0, 0, 0),
                                          pipeline_mode=pl.Buffered(1))],
        out_specs=[cur, pl.BlockSpec((None, rows, n_seq * LANES), lambda b, r, i: (b, i, r))],
        out_shape=[jax.ShapeDtypeStruct((batch, sub, dil * w), BF16),
                   jax.ShapeDtypeStruct((batch, sub, dil * LANES), F32)],
        compiler_params=pltpu.CompilerParams(
            dimension_semantics=("arbitrary", "arbitrary", "arbitrary"),
            vmem_limit_bytes=VMEM_LIMIT),
        name=f"dilated_attention_d{dil}",
    )(*operands, tables)
    return o.reshape(batch * sub, dil * w), lse.reshape(batch * sub, dil * LANES)


def _natural_order(ref, scr_ref, sub, dil, width):
    per = SUB_TILE // dil
    rows = slice(sub * per, (sub + 1) * per)
    if dil == 1:
        return ref[rows, :].astype(F32)
    n_col = width // LANES
    for r in range(dil):
        for c in range(n_col):
            lanes = slice(r * width + c * LANES, r * width + (c + 1) * LANES)
            scr_ref[c, pl.ds(r, per, stride=dil), :] = ref[rows, lanes].astype(F32)
    return jnp.concatenate([scr_ref[c] for c in range(n_col)], axis=1)


def _layer1_tail_chain(sub, expand, h_ref, o_refs, l_refs, wout_ref, mlpg_ref, wup_ref, wdn_ref,
                       fing_ref, out_ref, os_ref, ls_ref):
    rows = slice(sub * SUB_TILE, (sub + 1) * SUB_TILE)
    dils = [d for _, d in WINDOW_DILATIONS]
    lses = [_natural_order(l_ref, None if g == 0 else ls_ref.at[sub, g - 1], sub, dils[g], LANES)
            for g, l_ref in enumerate(l_refs)]
    outs = [_natural_order(o_ref, None if g == 0 else os_ref.at[sub, g - 1], sub, dils[g], ATT_WIDTH)
            for g, o_ref in enumerate(o_refs)]
    m_all = jnp.maximum(jnp.maximum(lses[0], lses[1]), lses[2])
    es = [jnp.exp2(l - m_all) for l in lses]
    inv = 1.0 / (es[0] + es[1] + es[2])
    o = outs[-1]
    for e, og in zip(es[:-1], outs[:-1]):
        o = o + _dot((e * inv).astype(BF16), expand) * (og - outs[-1])
    yield
    h1 = h_ref[rows, :] + _dot(o.astype(BF16), wout_ref[...])
    yield
    h2 = yield from _mlp_residual(h1, mlpg_ref[...], wup_ref, wdn_ref)
    out_ref[rows, :] = _rms_norm(h2, fing_ref[...])


def _layer1_tail_kernel(h_ref, o0_ref, o1_ref, o2_ref, l0_ref, l1_ref, l2_ref, *refs):
    row = lax.broadcasted_iota(jnp.int32, (LANES, ATT_WIDTH), 0)
    head = lax.broadcasted_iota(jnp.int32, (LANES, ATT_WIDTH), 1) // HEAD_DIM
    src = (head % HEADS_PER_VREG) * HEAD_DIM + HEADS_PER_VREG * (head // HEADS_PER_VREG)
    expand = jnp.where(row == src, 1.0, 0.0).astype(BF16)
    _interleave((_layer1_tail_chain(sub, expand, h_ref, (o0_ref, o1_ref, o2_ref), (l0_ref, l1_ref, l2_ref), *refs)
                 for sub in range(h_ref.shape[0] // SUB_TILE)), CHAIN_LAG)


def _layer1_tail(h, os_, lses, w_out, mlp_g, w_up, w_down, fin_g):
    t = h.shape[0]
    tile = pl.BlockSpec((TOKEN_TILE, D_MODEL), lambda i: (i, 0))
    otiles = [pl.BlockSpec((TOKEN_TILE // d, d * ATT_WIDTH), lambda i: (i, 0)) for _, d in WINDOW_DILATIONS]
    ltiles = [pl.BlockSpec((TOKEN_TILE // d, d * LANES), lambda i: (i, 0)) for _, d in WINDOW_DILATIONS]
    return pl.pallas_call(
        _layer1_tail_kernel,
        grid=(t // TOKEN_TILE,),
        in_specs=[tile, *otiles, *ltiles, _whole(w_out.shape),
                  _whole((1, D_MODEL)), _whole(w_up.shape), _whole(w_down.shape),
                  _whole((1, D_MODEL))],
        out_specs=tile,
        out_shape=jax.ShapeDtypeStruct((t, D_MODEL), F32),
        scratch_shapes=[
            pltpu.VMEM((TOKEN_TILE // SUB_TILE, N_DIL_GROUPS - 1, ATT_WIDTH // LANES, SUB_TILE, LANES), F32),
            pltpu.VMEM((TOKEN_TILE // SUB_TILE, N_DIL_GROUPS - 1, 1, SUB_TILE, LANES), F32)],
        compiler_params=pltpu.CompilerParams(dimension_semantics=("arbitrary",),
                                             vmem_limit_bytes=VMEM_LIMIT),
        name="layer1_merge_proj_mlp",
    )(h, *os_, *lses, w_out, mlp_g, w_up, w_down, fin_g)


def kernel(x, mix_norm_g, mlp_norm_g, final_norm_g, a_w_in, a_ln_g, a_ln_b, a_w_s, a_b_s, a_w_out,
           b_w_qkv, b_w_out, rel_bias, w_up, w_down):
    batch, seq, d = x.shape
    max_dil = WINDOW_DILATIONS[-1][1]
    assert d == D_MODEL and seq % (BLK * max_dil) == 0
    assert seq % TOKEN_TILE == 0 and TOKEN_TILE % SUB_TILE == 0
    assert SUB_TILE % (16 * max_dil) == 0 and TOKEN_TILE % LAYER0_SUB_TILE == 0 and LAYER0_SUB_TILE % CHUNK == 0
    row = lambda a: a.reshape(1, -1).astype(F32)
    h = x.reshape(batch * seq, d)

    tables, w_in0, w_out0, w_up0, w_down0 = _prep(rel_bias, [(a_w_in, 0), (a_w_out, 0), (w_up, 0), (w_down, 0)])
    h, w_qkv = _layer0(h, row(mix_norm_g[0]), w_in0, row(a_ln_g[0]), row(a_ln_b[0]), a_w_s[0], a_b_s[0].T,
                       w_out0, row(mlp_norm_g[0]), w_up0, w_down0, (b_w_qkv, 0))

    *qkv, w_up1, w_down1, w_out1 = _qkv(h, row(mix_norm_g[1]), w_qkv, [(w_up, 1), (w_down, 1), (b_w_out, 0)])
    os_, lses = [], []
    for g, (_, dil) in enumerate(WINDOW_DILATIONS):
        o, lse = _attention_group(qkv[g], qkv[N_DIL_GROUPS + g], qkv[2 * N_DIL_GROUPS + g],
                                  tables, g, batch, seq, dil, ATTN_BLOCKS_PER_STEP)
        os_.append(o)
        lses.append(lse)

    out = _layer1_tail(h, os_, lses, w_out1, row(mlp_norm_g[1]), w_up1, w_down1, row(final_norm_g))
    return out.reshape(batch, seq, d)
```

```python
import functools
import math

import jax
import jax.numpy as jnp
from jax import lax
from jax.experimental import pallas as pl
from jax.experimental.pallas import tpu as pltpu

D_MODEL = 1024
CHUNK = 128
GATE_WIDTH = D_MODEL
GATE_GROUPS = 8
WINDOW_DILATIONS = ((128, 1), (512, 4), (2048, 16))
N_DIL_GROUPS = len(WINDOW_DILATIONS)
ATT_HEADS = 8
HEAD_DIM = 64
ATT_WIDTH = ATT_HEADS * HEAD_DIM
N_BUCKETS = 32
MAX_EXACT = N_BUCKETS // 2
REL_MAX_DISTANCE = max(w for w, _ in WINDOW_DILATIONS)
D_FF = 4 * D_MODEL
EPS = 1e-6
NEG_INF = -1e30
LOG2E = math.log2(math.e)

BLK = 128
LANES = 128
HEADS_PER_VREG = LANES // HEAD_DIM
HEAD_PAIRS = ATT_HEADS // HEADS_PER_VREG
TOKEN_TILE = 1024
SUB_TILE = 512
LAYER0_SUB_TILE = 256
CHAIN_LAG = 2
FF_CHUNK = 1024
ATTN_BLOCKS_PER_STEP = 16
PREP_HEADS_PER_STEP = 3
VMEM_LIMIT = 63 * 1024 * 1024
GAIN_ROWS = 8

F32 = jnp.float32
BF16 = jnp.bfloat16


def _dot(a, b):
    return jnp.dot(a, b, preferred_element_type=F32)


def _dot_nt(a, b):
    return lax.dot_general(a, b, (((1,), (1,)), ((), ())), preferred_element_type=F32)


def _gain(ref):
    return ref[0:1, :]


def _rms_norm(x, g):
    return x * lax.rsqrt(jnp.mean(x * x, axis=-1, keepdims=True) + EPS) * g


def _gelu_exact(x):
    return 0.5 * x * (1.0 + lax.erf(x * math.sqrt(0.5)))


def _interleave(chains, lag):
    chains = list(chains)
    alive = [True] * len(chains)
    t = 0
    while any(alive):
        for i, chain in enumerate(chains):
            if alive[i] and t >= i * lag:
                try:
                    next(chain)
                except StopIteration:
                    alive[i] = False
        t += 1


def _mlp_residual(h, g, wup_ref, wdn_ref):
    xn = _rms_norm(h, g).astype(BF16)
    acc = h
    for c in range(D_FF // FF_CHUNK):
        cols = slice(c * FF_CHUNK, (c + 1) * FF_CHUNK)
        up = _dot(xn, wup_ref[:, cols])
        yield
        act = jnp.square(jnp.maximum(up, 0.0)).astype(BF16)
        acc = acc + _dot(act, wdn_ref[cols, :])
        yield
    return acc


def _whole(shape):
    return pl.BlockSpec(shape, lambda *_: (0,) * len(shape))


def _cast_rider_specs(mats, n_steps):
    in_specs, out_specs, out_shape = [], [], []
    for arr, layer in mats:
        k, n = arr.shape[-2:]
        rows = k // n_steps
        assert k % n_steps == 0 and rows % 16 == 0
        in_specs.append(pl.BlockSpec((None, rows, n), lambda i, layer=layer: (layer, i, 0)))
        out_specs.append(pl.BlockSpec((rows, n), lambda i: (i, 0)))
        out_shape.append(jax.ShapeDtypeStruct((k, n), BF16))
    return in_specs, out_specs, out_shape


def _cast_riders(src_refs, dst_refs):
    for src, dst in zip(src_refs, dst_refs):
        dst[...] = src[...].astype(BF16)


def _layer0_chain(rows, causal, h_ref, mixg_ref, win_ref, lng_ref, lnb_ref, ws_ref, bst_ref,
                  wout_ref, mlpg_ref, wup_ref, wdn_ref, out_ref):
    nc = LAYER0_SUB_TILE // CHUNK
    h = h_ref[rows, :]
    xn = _rms_norm(h, _gain(mixg_ref)).astype(BF16)
    v = _gelu_exact(_dot(xn, win_ref[:, GATE_WIDTH:]))
    yield
    u = _gelu_exact(_dot(xn, win_ref[:, :GATE_WIDTH]))
    yield
    mu = jnp.mean(v, axis=-1, keepdims=True)
    vc = v - mu
    vn = vc * lax.rsqrt(jnp.mean(vc * vc, axis=-1, keepdims=True) + EPS)
    vb = (vn * _gain(lng_ref) + _gain(lnb_ref)).astype(BF16)
    mixed = []
    for g in range(GATE_GROUPS):
        wg = jnp.where(causal, ws_ref[g], 0.0).astype(BF16)
        lanes = slice(g * CHUNK, (g + 1) * CHUNK)
        vg = jnp.concatenate([vb[c * CHUNK:(c + 1) * CHUNK, lanes] for c in range(nc)], axis=1)
        mixed.append(_dot(wg, vg) + bst_ref[:, g:g + 1])
    gate = jnp.concatenate(
        [jnp.concatenate([mixed[g][:, c * CHUNK:(c + 1) * CHUNK] for g in range(GATE_GROUPS)], axis=1)
         for c in range(nc)], axis=0)
    yield
    h1 = h + _dot((u * gate).astype(BF16), wout_ref[...])
    yield
    out_ref[rows, :] = yield from _mlp_residual(h1, _gain(mlpg_ref), wup_ref, wdn_ref)


def _layer0_kernel(h_ref, *refs):
    *weights, cast_src, out_ref, cast_dst = refs
    t_idx = lax.broadcasted_iota(jnp.int32, (CHUNK, CHUNK), 0)
    s_idx = lax.broadcasted_iota(jnp.int32, (CHUNK, CHUNK), 1)
    causal = s_idx <= t_idx
    sub_tile = LAYER0_SUB_TILE
    _interleave((_layer0_chain(slice(sub * sub_tile, (sub + 1) * sub_tile), causal, h_ref, *weights, out_ref)
                 for sub in range(h_ref.shape[0] // sub_tile)), CHAIN_LAG)
    _cast_riders([cast_src], [cast_dst])


def _layer0(h, mix_g, w_in, ln_g, ln_b, w_s, b_st, w_out, mlp_g, w_up, w_down, to_cast):
    t = h.shape[0]
    n_steps = t // TOKEN_TILE
    tile = pl.BlockSpec((TOKEN_TILE, D_MODEL), lambda i: (i, 0))
    cast_in, cast_out, cast_shape = _cast_rider_specs([to_cast], n_steps)
    return pl.pallas_call(
        _layer0_kernel,
        grid=(n_steps,),
        in_specs=[tile, _whole(mix_g.shape), _whole(w_in.shape), _whole(ln_g.shape),
                  _whole(ln_b.shape), _whole(w_s.shape), _whole(b_st.shape), _whole(w_out.shape),
                  _whole(mlp_g.shape), _whole(w_up.shape), _whole(w_down.shape)] + cast_in,
        out_specs=[tile] + cast_out,
        out_shape=[jax.ShapeDtypeStruct((t, D_MODEL), F32)] + cast_shape,
        compiler_params=pltpu.CompilerParams(dimension_semantics=("arbitrary",),
                                             vmem_limit_bytes=VMEM_LIMIT),
        name="layer0_gating_mlp",
    )(h, mix_g, w_in, ln_g, ln_b, w_s, b_st, w_out, mlp_g, w_up, w_down, to_cast[0])


def _qkv_chain(sub, h_ref, g_ref, w_ref, out_refs, xs_ref):
    n_col = D_MODEL // LANES
    xn = _rms_norm(h_ref[sub * SUB_TILE:(sub + 1) * SUB_TILE, :], _gain(g_ref))
    for c in range(n_col):
        xs_ref[sub, c] = xn[:, c * LANES:(c + 1) * LANES]
    for grp, (_, dil) in enumerate(WINDOW_DILATIONS):
        per = SUB_TILE // dil
        if dil == 1:
            x = xn
        else:
            x = jnp.concatenate(
                [jnp.concatenate([xs_ref[sub, c, pl.ds(r, per, stride=dil), :] for c in range(n_col)],
                                 axis=1) for r in range(dil)], axis=0)
        x = x.astype(BF16)
        for part in range(3):
            j = part * N_DIL_GROUPS + grp
            y = _dot(x, w_ref[:, j * ATT_WIDTH:(j + 1) * ATT_WIDTH])
            if part == 0:
                y = y * (HEAD_DIM ** -0.5 * LOG2E)
            y = y.astype(BF16)
            o_ref = out_refs[j]
            for r in range(dil):
                o_ref[sub * per:(sub + 1) * per, r * ATT_WIDTH:(r + 1) * ATT_WIDTH] = y[r * per:(r + 1) * per, :]
            yield


def _qkv_kernel(h_ref, g_ref, w_ref, *refs, n_cast):
    n_out = 3 * N_DIL_GROUPS
    cast_src, out_refs = refs[:n_cast], refs[n_cast:n_cast + n_out]
    cast_dst, xs_ref = refs[n_cast + n_out:-1], refs[-1]
    for sub in range(h_ref.shape[0] // SUB_TILE):
        for _ in _qkv_chain(sub, h_ref, g_ref, w_ref, out_refs, xs_ref):
            pass
    _cast_riders(cast_src, cast_dst)


def _qkv(h, g, w_qkv, to_cast):
    t = h.shape[0]
    n_steps = t // TOKEN_TILE
    tile = pl.BlockSpec((TOKEN_TILE, D_MODEL), lambda i: (i, 0))
    cast_in, cast_out, cast_shape = _cast_rider_specs(to_cast, n_steps)
    out_specs, out_shape = [], []
    for _ in range(3):
        for _, dil in WINDOW_DILATIONS:
            out_specs.append(pl.BlockSpec((TOKEN_TILE // dil, dil * ATT_WIDTH), lambda i: (i, 0)))
            out_shape.append(jax.ShapeDtypeStruct((t // dil, dil * ATT_WIDTH), BF16))
    return pl.pallas_call(
        functools.partial(_qkv_kernel, n_cast=len(to_cast)),
        grid=(n_steps,),
        in_specs=[tile, _whole(g.shape), _whole(w_qkv.shape)] + cast_in,
        out_specs=out_specs + cast_out,
        out_shape=out_shape + cast_shape,
        scratch_shapes=[pltpu.VMEM((TOKEN_TILE // SUB_TILE, D_MODEL // LANES, SUB_TILE, LANES), F32)],
        compiler_params=pltpu.CompilerParams(dimension_semantics=("arbitrary",),
                                             vmem_limit_bytes=VMEM_LIMIT),
        name="qkv_proj",
    )(h, g, w_qkv, *[arr for arr, _ in to_cast])


def _t5_bucket(distance):
    small = distance < MAX_EXACT
    nf = jnp.maximum(distance, 1).astype(F32)
    large = MAX_EXACT + (jnp.log(nf / MAX_EXACT) / math.log(REL_MAX_DISTANCE / MAX_EXACT)
                         * (N_BUCKETS - MAX_EXACT)).astype(jnp.int32)
    large = jnp.minimum(large, N_BUCKETS - 1)
    return jnp.where(small, distance, large)


def _prep_kernel(rb_ref, bucket_ref, *refs, n_cast):
    cast_src, tables_ref, cast_dst = refs[:n_cast], refs[n_cast], refs[n_cast + 1:]
    _cast_riders(cast_src, cast_dst)
    i_idx = lax.broadcasted_iota(jnp.int32, (BLK, 2 * BLK), 0)
    j_idx = lax.broadcasted_iota(jnp.int32, (BLK, 2 * BLK), 1)
    rel = BLK + i_idx - j_idx
    band = (rel >= 0) & (rel <= BLK)
    for k in range(PREP_HEADS_PER_STEP):
        gh = pl.program_id(0) * PREP_HEADS_PER_STEP + k
        g, h = gh // ATT_HEADS, gh % ATT_HEADS
        bucket = bucket_ref[g]
        acc = jnp.zeros((BLK, 2 * BLK), F32)
        for b in range(N_BUCKETS):
            acc = jnp.where(bucket == b, rb_ref[b, gh], acc)
        pair = h // HEADS_PER_VREG
        rows = pl.ds(pl.multiple_of((h % HEADS_PER_VREG) * BLK, BLK), BLK)
        tables_ref[g, 0, pair, rows, :] = jnp.where(band, acc * LOG2E, NEG_INF)
        tables_ref[g, 1, pair, rows, :] = jnp.where(band & (j_idx >= BLK), acc * LOG2E, NEG_INF)


def _prep(rel_bias, to_cast):
    n_steps = N_DIL_GROUPS * ATT_HEADS // PREP_HEADS_PER_STEP
    rel = BLK + jnp.arange(BLK)[:, None] - jnp.arange(2 * BLK)[None, :]
    buckets = jnp.stack([_t5_bucket(jnp.clip(rel, 0, BLK) * dil) for _, dil in WINDOW_DILATIONS])
    shape = (N_DIL_GROUPS, 2, HEAD_PAIRS, HEADS_PER_VREG * BLK, 2 * BLK)
    cast_in, cast_out, cast_shape = _cast_rider_specs(to_cast, n_steps)
    return pl.pallas_call(
        functools.partial(_prep_kernel, n_cast=len(to_cast)),
        grid=(n_steps,),
        in_specs=[pl.BlockSpec(memory_space=pltpu.SMEM), _whole(buckets.shape)] + cast_in,
        out_specs=[_whole(shape)] + cast_out,
        out_shape=[jax.ShapeDtypeStruct(shape, F32)] + cast_shape,
        compiler_params=pltpu.CompilerParams(dimension_semantics=("arbitrary",),
                                             vmem_limit_bytes=VMEM_LIMIT),
        name="bias_tables_and_casts",
    )(rel_bias.astype(F32), buckets.astype(jnp.int32), *[arr for arr, _ in to_cast])


def _lse_lane(head):
    return (head % HEADS_PER_VREG) * HEAD_DIM + HEADS_PER_VREG * (head // HEADS_PER_VREG)


def _attn_group_body(in_refs, tb_ref, out_refs, whole_sequence, seq_step):
    if whole_sequence:
        q_ref, k_ref, v_ref = in_refs
    else:
        q_ref, k_ref, v_ref, kp_ref, vp_ref = in_refs
    o_ref, lse_ref = out_refs
    rows = q_ref.shape[0]

    def unit(q, keys, vals, variant):
        lane = lax.broadcasted_iota(jnp.int32, (BLK, LANES), 1)
        first = lane < HEAD_DIM
        ones = jnp.ones((2 * BLK, LANES), BF16)
        outs = []
        lse = jnp.zeros((BLK, LANES), F32)
        for pair in range(HEAD_PAIRS):
            sl = slice(pair * LANES, (pair + 1) * LANES)
            qp, kp, vp = q[:, sl], keys[:, sl], vals[:, sl]
            zero = jnp.zeros_like(qp)
            qq = jnp.concatenate([jnp.where(first, qp, zero), jnp.where(first, zero, qp)], axis=0)
            logits = _dot_nt(qq, kp) + tb_ref[variant, pair]
            m = jnp.max(logits, axis=-1, keepdims=True)
            p = jnp.exp2(logits - m).astype(BF16)
            res = _dot(p, jnp.concatenate([vp, ones], axis=1))
            num = jnp.where(first, res[:BLK, :LANES], res[BLK:, :LANES])
            den = jnp.where(first, res[:BLK, LANES:], res[BLK:, LANES:])
            outs.append((num / den).astype(BF16))
            lse_pair = jnp.where(first, m[:BLK], m[BLK:]) + jnp.log2(den)
            keep = (lane == _lse_lane(pair * HEADS_PER_VREG)) | (lane == _lse_lane(pair * HEADS_PER_VREG + 1))
            lse = jnp.where(keep, lse_pair, lse)
        return jnp.concatenate(outs, axis=1), lse

    first_variant = 1 if whole_sequence else jnp.where(seq_step == 0, 1, 0)
    for s in range(q_ref.shape[1] // ATT_WIDTH):
        cols = slice(s * ATT_WIDTH, (s + 1) * ATT_WIDTH)
        lse_cols = slice(s * LANES, (s + 1) * LANES)
        for u in range(rows // BLK):
            cur_rows = slice(u * BLK, (u + 1) * BLK)
            if u > 0:
                both = slice((u - 1) * BLK, (u + 1) * BLK)
                o, lse = unit(q_ref[cur_rows, cols], k_ref[both, cols], v_ref[both, cols], 0)
            else:
                k_before = k_ref[cur_rows, cols] if whole_sequence else kp_ref[:, cols]
                v_before = v_ref[cur_rows, cols] if whole_sequence else vp_ref[:, cols]
                keys = jnp.concatenate([k_before, k_ref[cur_rows, cols]], axis=0)
                vals = jnp.concatenate([v_before, v_ref[cur_rows, cols]], axis=0)
                o, lse = unit(q_ref[cur_rows, cols], keys, vals, first_variant)
            o_ref[cur_rows, cols] = o
            lse_ref[cur_rows, lse_cols] = lse


def _attn_kernel(*refs, whole_flags):
    pos, ins = 0, []
    for whole in whole_flags:
        n = 3 if whole else 5
        ins.append(refs[pos:pos + n])
        pos += n
    tables_ref = refs[pos]
    outs = refs[pos + 1:]
    for g, whole in enumerate(whole_flags):
        _attn_group_body(ins[g], tables_ref.at[g], outs[2 * g:2 * g + 2], whole, pl.program_id(1))


def _attention(qkv, tables, batch, seq):
    w = ATT_WIDTH
    operands, in_specs, out_specs, out_shape, whole_flags, inner = [], [], [], [], [], None
    for g, (_, dil) in enumerate(WINDOW_DILATIONS):
        sub = seq // dil
        rows = min(ATTN_BLOCKS_PER_STEP * BLK, sub)
        n_seq = ATTN_BLOCKS_PER_STEP * BLK // rows
        assert sub % rows == 0 and dil % n_seq == 0
        whole = rows == sub
        steps = (dil // n_seq) * (sub // rows)
        assert (whole or dil == n_seq) and inner in (None, steps)
        inner = steps
        q, k, v = (qkv[part * N_DIL_GROUPS + g].reshape(batch, sub, dil * w) for part in range(3))
        if whole:
            cur = pl.BlockSpec((None, rows, n_seq * w), lambda b, j: (b, 0, j))
            lse_spec = pl.BlockSpec((None, rows, n_seq * LANES), lambda b, j: (b, 0, j))
            operands += [q, k, v]
            in_specs += [cur, cur, cur]
        else:
            bps = rows // BLK
            cur = pl.BlockSpec((None, rows, n_seq * w), lambda b, j: (b, j, 0))
            lse_spec = pl.BlockSpec((None, rows, n_seq * LANES), lambda b, j: (b, j, 0))
            prev = pl.BlockSpec((None, BLK, n_seq * w), lambda b, j, bps=bps: (b, jnp.maximum(j * bps - 1, 0), 0))
            operands += [q, k, v, k, v]
            in_specs += [cur, cur, cur, prev, prev]
        whole_flags.append(whole)
        out_specs += [cur, lse_spec]
        out_shape += [jax.ShapeDtypeStruct((batch, sub, dil * w), BF16),
                      jax.ShapeDtypeStruct((batch, sub, dil * LANES), F32)]
    res = pl.pallas_call(
        functools.partial(_attn_kernel, whole_flags=tuple(whole_flags)),
        grid=(batch, inner),
        in_specs=in_specs + [_whole(tables.shape)],
        out_specs=out_specs,
        out_shape=out_shape,
        compiler_params=pltpu.CompilerParams(dimension_semantics=("arbitrary", "arbitrary"),
                                             vmem_limit_bytes=VMEM_LIMIT),
        name="dilated_attention",
    )(*operands, tables)
    os_ = [o.reshape(-1, o.shape[-1]) for o in res[0::2]]
    lses = [l.reshape(-1, l.shape[-1]) for l in res[1::2]]
    return os_, lses


def _natural_order(ref, scr_ref, sub, dil, width):
    per = SUB_TILE // dil
    rows = slice(sub * per, (sub + 1) * per)
    if dil == 1:
        return ref[rows, :].astype(F32)
    n_col = width // LANES
    for r in range(dil):
        for c in range(n_col):
            lanes = slice(r * width + c * LANES, r * width + (c + 1) * LANES)
            scr_ref[c, pl.ds(r, per, stride=dil), :] = ref[rows, lanes].astype(F32)
    return jnp.concatenate([scr_ref[c] for c in range(n_col)], axis=1)


def _layer1_tail_chain(sub, expand, h_ref, o_refs, l_refs, wout_ref, mlpg_ref, wup_ref, wdn_ref,
                       fing_ref, out_ref, os_ref, ls_ref):
    rows = slice(sub * SUB_TILE, (sub + 1) * SUB_TILE)
    dils = [d for _, d in WINDOW_DILATIONS]
    lses = [_natural_order(l_ref, None if g == 0 else ls_ref.at[sub, g - 1], sub, dils[g], LANES)
            for g, l_ref in enumerate(l_refs)]
    outs = [_natural_order(o_ref, None if g == 0 else os_ref.at[sub, g - 1], sub, dils[g], ATT_WIDTH)
            for g, o_ref in enumerate(o_refs)]
    m_all = jnp.maximum(jnp.maximum(lses[0], lses[1]), lses[2])
    es = [jnp.exp2(l - m_all) for l in lses]
    inv = 1.0 / (es[0] + es[1] + es[2])
    o = outs[-1]
    for e, og in zip(es[:-1], outs[:-1]):
        o = o + _dot((e * inv).astype(BF16), expand) * (og - outs[-1])
    yield
    h1 = h_ref[rows, :] + _dot(o.astype(BF16), wout_ref[...])
    yield
    h2 = yield from _mlp_residual(h1, _gain(mlpg_ref), wup_ref, wdn_ref)
    out_ref[rows, :] = _rms_norm(h2, _gain(fing_ref))


def _layer1_tail_kernel(h_ref, o0_ref, o1_ref, o2_ref, l0_ref, l1_ref, l2_ref, *refs):
    row = lax.broadcasted_iota(jnp.int32, (LANES, ATT_WIDTH), 0)
    head = lax.broadcasted_iota(jnp.int32, (LANES, ATT_WIDTH), 1) // HEAD_DIM
    src = (head % HEADS_PER_VREG) * HEAD_DIM + HEADS_PER_VREG * (head // HEADS_PER_VREG)
    expand = jnp.where(row == src, 1.0, 0.0).astype(BF16)
    _interleave((_layer1_tail_chain(sub, expand, h_ref, (o0_ref, o1_ref, o2_ref), (l0_ref, l1_ref, l2_ref), *refs)
                 for sub in range(h_ref.shape[0] // SUB_TILE)), CHAIN_LAG)


def _layer1_tail(h, os_, lses, w_out, mlp_g, w_up, w_down, fin_g):
    t = h.shape[0]
    tile = pl.BlockSpec((TOKEN_TILE, D_MODEL), lambda i: (i, 0))
    otiles = [pl.BlockSpec((TOKEN_TILE // d, d * ATT_WIDTH), lambda i: (i, 0)) for _, d in WINDOW_DILATIONS]
    ltiles = [pl.BlockSpec((TOKEN_TILE // d, d * LANES), lambda i: (i, 0)) for _, d in WINDOW_DILATIONS]
    return pl.pallas_call(
        _layer1_tail_kernel,
        grid=(t // TOKEN_TILE,),
        in_specs=[tile, *otiles, *ltiles, _whole(w_out.shape),
                  _whole(mlp_g.shape), _whole(w_up.shape), _whole(w_down.shape),
                  _whole(fin_g.shape)],
        out_specs=tile,
        out_shape=jax.ShapeDtypeStruct((t, D_MODEL), F32),
        scratch_shapes=[
            pltpu.VMEM((TOKEN_TILE // SUB_TILE, N_DIL_GROUPS - 1, ATT_WIDTH // LANES, SUB_TILE, LANES), F32),
            pltpu.VMEM((TOKEN_TILE // SUB_TILE, N_DIL_GROUPS - 1, 1, SUB_TILE, LANES), F32)],
        compiler_params=pltpu.CompilerParams(dimension_semantics=("arbitrary",),
                                             vmem_limit_bytes=VMEM_LIMIT),
        name="layer1_merge_proj_mlp",
    )(h, *os_, *lses, w_out, mlp_g, w_up, w_down, fin_g)


def kernel(x, mix_norm_g, mlp_norm_g, final_norm_g, a_w_in, a_ln_g, a_ln_b, a_w_s, a_b_s, a_w_out,
           b_w_qkv, b_w_out, rel_bias, w_up, w_down):
    batch, seq, d = x.shape
    max_dil = WINDOW_DILATIONS[-1][1]
    assert d == D_MODEL and seq % (BLK * max_dil) == 0
    assert seq % TOKEN_TILE == 0 and TOKEN_TILE % SUB_TILE == 0
    assert SUB_TILE % (16 * max_dil) == 0 and TOKEN_TILE % LAYER0_SUB_TILE == 0 and LAYER0_SUB_TILE % CHUNK == 0
    row = lambda a: jnp.broadcast_to(a.reshape(1, -1).astype(F32), (GAIN_ROWS, a.size))
    h = x.reshape(batch * seq, d)

    tables, w_in0, w_out0, w_up0, w_down0 = _prep(rel_bias, [(a_w_in, 0), (a_w_out, 0), (w_up, 0), (w_down, 0)])
    h, w_qkv = _layer0(h, row(mix_norm_g[0]), w_in0, row(a_ln_g[0]), row(a_ln_b[0]), a_w_s[0], a_b_s[0].T,
                       w_out0, row(mlp_norm_g[0]), w_up0, w_down0, (b_w_qkv, 0))

    *qkv, w_up1, w_down1, w_out1 = _qkv(h, row(mix_norm_g[1]), w_qkv, [(w_up, 1), (w_down, 1), (b_w_out, 0)])
    os_, lses = _attention(qkv, tables, batch, seq)

    out = _layer1_tail(h, os_, lses, w_out1, row(mlp_norm_g[1]), w_up1, w_down1, row(final_norm_g))
    return out.reshape(batch, seq, d)
```

```python
import functools
import math

import jax
import jax.numpy as jnp
from jax import lax
from jax.experimental import pallas as pl
from jax.experimental.pallas import tpu as pltpu

D_MODEL = 1024
CHUNK = 128
GATE_WIDTH = D_MODEL
GATE_GROUPS = 8
WINDOW_DILATIONS = ((128, 1), (512, 4), (2048, 16))
N_DIL_GROUPS = len(WINDOW_DILATIONS)
ATT_HEADS = 8
HEAD_DIM = 64
ATT_WIDTH = ATT_HEADS * HEAD_DIM
N_BUCKETS = 32
MAX_EXACT = N_BUCKETS // 2
REL_MAX_DISTANCE = max(w for w, _ in WINDOW_DILATIONS)
D_FF = 4 * D_MODEL
EPS = 1e-6
NEG_INF = -1e30
LOG2E = math.log2(math.e)

BLK = 128
LANES = 128
HEADS_PER_VREG = LANES // HEAD_DIM
HEAD_PAIRS = ATT_HEADS // HEADS_PER_VREG
TOKEN_TILE = 1024
SUB_TILE = 512
LAYER0_SUB_TILE = 256
CHAIN_LAG = 2
FF_CHUNK = 1024
ATTN_BLOCKS_PER_STEP = 16
PREP_HEADS_PER_STEP = 3
VMEM_LIMIT = 63 * 1024 * 1024

F32 = jnp.float32
BF16 = jnp.bfloat16


def _dot(a, b):
    return jnp.dot(a, b, preferred_element_type=F32)


def _dot_nt(a, b):
    return lax.dot_general(a, b, (((1,), (1,)), ((), ())), preferred_element_type=F32)


def _rms_norm(x, g):
    return x * lax.rsqrt(jnp.mean(x * x, axis=-1, keepdims=True) + EPS) * g


def _gelu_exact(x):
    return 0.5 * x * (1.0 + lax.erf(x * math.sqrt(0.5)))


def _interleave(chains, lag):
    chains = list(chains)
    alive = [True] * len(chains)
    t = 0
    while any(alive):
        for i, chain in enumerate(chains):
            if alive[i] and t >= i * lag:
                try:
                    next(chain)
                except StopIteration:
                    alive[i] = False
        t += 1


def _mlp_residual(h, g, wup_ref, wdn_ref):
    xn = _rms_norm(h, g).astype(BF16)
    acc = h
    for c in range(D_FF // FF_CHUNK):
        cols = slice(c * FF_CHUNK, (c + 1) * FF_CHUNK)
        up = _dot(xn, wup_ref[:, cols])
        yield
        act = jnp.square(jnp.maximum(up, 0.0)).astype(BF16)
        acc = acc + _dot(act, wdn_ref[cols, :])
        yield
    return acc


def _whole(shape):
    return pl.BlockSpec(shape, lambda *_: (0,) * len(shape))


def _cast_rider_specs(mats, n_steps):
    in_specs, out_specs, out_shape = [], [], []
    for arr, layer in mats:
        k, n = arr.shape[-2:]
        rows = k // n_steps
        assert k % n_steps == 0 and rows % 16 == 0
        in_specs.append(pl.BlockSpec((None, rows, n), lambda i, layer=layer: (layer, i, 0)))
        out_specs.append(pl.BlockSpec((rows, n), lambda i: (i, 0)))
        out_shape.append(jax.ShapeDtypeStruct((k, n), BF16))
    return in_specs, out_specs, out_shape


def _cast_riders(src_refs, dst_refs):
    for src, dst in zip(src_refs, dst_refs):
        dst[...] = src[...].astype(BF16)


def _layer0_chain(rows, causal, h_ref, mixg_ref, win_ref, lng_ref, lnb_ref, ws_ref, bst_ref,
                  wout_ref, mlpg_ref, wup_ref, wdn_ref, out_ref):
    nc = LAYER0_SUB_TILE // CHUNK
    h = h_ref[rows, :]
    xn = _rms_norm(h, mixg_ref[...]).astype(BF16)
    v = _gelu_exact(_dot(xn, win_ref[:, GATE_WIDTH:]))
    yield
    u = _gelu_exact(_dot(xn, win_ref[:, :GATE_WIDTH]))
    yield
    mu = jnp.mean(v, axis=-1, keepdims=True)
    vc = v - mu
    vn = vc * lax.rsqrt(jnp.mean(vc * vc, axis=-1, keepdims=True) + EPS)
    vb = (vn * lng_ref[...] + lnb_ref[...]).astype(BF16)
    mixed = []
    for g in range(GATE_GROUPS):
        wg = jnp.where(causal, ws_ref[g], 0.0).astype(BF16)
        lanes = slice(g * CHUNK, (g + 1) * CHUNK)
        vg = jnp.concatenate([vb[c * CHUNK:(c + 1) * CHUNK, lanes] for c in range(nc)], axis=1)
        mixed.append(_dot(wg, vg) + bst_ref[:, g:g + 1])
    gate = jnp.concatenate(
        [jnp.concatenate([mixed[g][:, c * CHUNK:(c + 1) * CHUNK] for g in range(GATE_GROUPS)], axis=1)
         for c in range(nc)], axis=0)
    yield
    h1 = h + _dot((u * gate).astype(BF16), wout_ref[...])
    yield
    out_ref[rows, :] = yield from _mlp_residual(h1, mlpg_ref[...], wup_ref, wdn_ref)


def _layer0_kernel(h_ref, *refs):
    *weights, cast_src, out_ref, cast_dst = refs
    t_idx = lax.broadcasted_iota(jnp.int32, (CHUNK, CHUNK), 0)
    s_idx = lax.broadcasted_iota(jnp.int32, (CHUNK, CHUNK), 1)
    causal = s_idx <= t_idx
    sub_tile = LAYER0_SUB_TILE
    _interleave((_layer0_chain(slice(sub * sub_tile, (sub + 1) * sub_tile), causal, h_ref, *weights, out_ref)
                 for sub in range(h_ref.shape[0] // sub_tile)), CHAIN_LAG)
    _cast_riders([cast_src], [cast_dst])


def _layer0(h, mix_g, w_in, ln_g, ln_b, w_s, b_st, w_out, mlp_g, w_up, w_down, to_cast):
    t = h.shape[0]
    n_steps = t // TOKEN_TILE
    tile = pl.BlockSpec((TOKEN_TILE, D_MODEL), lambda i: (i, 0))
    cast_in, cast_out, cast_shape = _cast_rider_specs([to_cast], n_steps)
    return pl.pallas_call(
        _layer0_kernel,
        grid=(n_steps,),
        in_specs=[tile, _whole((1, D_MODEL)), _whole(w_in.shape), _whole((1, GATE_WIDTH)),
                  _whole((1, GATE_WIDTH)), _whole(w_s.shape), _whole(b_st.shape), _whole(w_out.shape),
                  _whole((1, D_MODEL)), _whole(w_up.shape), _whole(w_down.shape)] + cast_in,
        out_specs=[tile] + cast_out,
        out_shape=[jax.ShapeDtypeStruct((t, D_MODEL), F32)] + cast_shape,
        compiler_params=pltpu.CompilerParams(dimension_semantics=("arbitrary",),
                                             vmem_limit_bytes=VMEM_LIMIT),
        name="layer0_gating_mlp",
    )(h, mix_g, w_in, ln_g, ln_b, w_s, b_st, w_out, mlp_g, w_up, w_down, to_cast[0])


def _qkv_chain(sub, h_ref, g_ref, w_ref, out_refs, xs_ref):
    n_col = D_MODEL // LANES
    xn = _rms_norm(h_ref[sub * SUB_TILE:(sub + 1) * SUB_TILE, :], g_ref[...])
    for c in range(n_col):
        xs_ref[sub, c] = xn[:, c * LANES:(c + 1) * LANES]
    for grp, (_, dil) in enumerate(WINDOW_DILATIONS):
        per = SUB_TILE // dil
        if dil == 1:
            x = xn
        else:
            x = jnp.concatenate(
                [jnp.concatenate([xs_ref[sub, c, pl.ds(r, per, stride=dil), :] for c in range(n_col)],
                                 axis=1) for r in range(dil)], axis=0)
        x = x.astype(BF16)
        for part in range(3):
            j = part * N_DIL_GROUPS + grp
            y = _dot(x, w_ref[:, j * ATT_WIDTH:(j + 1) * ATT_WIDTH])
            if part == 0:
                y = y * (HEAD_DIM ** -0.5 * LOG2E)
            y = y.astype(BF16)
            o_ref = out_refs[j]
            for r in range(dil):
                o_ref[sub * per:(sub + 1) * per, r * ATT_WIDTH:(r + 1) * ATT_WIDTH] = y[r * per:(r + 1) * per, :]
            yield


def _qkv_kernel(h_ref, g_ref, w_ref, *refs, n_cast):
    n_out = 3 * N_DIL_GROUPS
    cast_src, out_refs = refs[:n_cast], refs[n_cast:n_cast + n_out]
    cast_dst, xs_ref = refs[n_cast + n_out:-1], refs[-1]
    for sub in range(h_ref.shape[0] // SUB_TILE):
        for _ in _qkv_chain(sub, h_ref, g_ref, w_ref, out_refs, xs_ref):
            pass
    _cast_riders(cast_src, cast_dst)


def _qkv(h, g, w_qkv, to_cast):
    t = h.shape[0]
    n_steps = t // TOKEN_TILE
    tile = pl.BlockSpec((TOKEN_TILE, D_MODEL), lambda i: (i, 0))
    cast_in, cast_out, cast_shape = _cast_rider_specs(to_cast, n_steps)
    out_specs, out_shape = [], []
    for _ in range(3):
        for _, dil in WINDOW_DILATIONS:
            out_specs.append(pl.BlockSpec((TOKEN_TILE // dil, dil * ATT_WIDTH), lambda i: (i, 0)))
            out_shape.append(jax.ShapeDtypeStruct((t // dil, dil * ATT_WIDTH), BF16))
    return pl.pallas_call(
        functools.partial(_qkv_kernel, n_cast=len(to_cast)),
        grid=(n_steps,),
        in_specs=[tile, _whole((1, D_MODEL)), _whole(w_qkv.shape)] + cast_in,
        out_specs=out_specs + cast_out,
        out_shape=out_shape + cast_shape,
        scratch_shapes=[pltpu.VMEM((TOKEN_TILE // SUB_TILE, D_MODEL // LANES, SUB_TILE, LANES), F32)],
        compiler_params=pltpu.CompilerParams(dimension_semantics=("arbitrary",),
                                             vmem_limit_bytes=VMEM_LIMIT),
        name="qkv_proj",
    )(h, g, w_qkv, *[arr for arr, _ in to_cast])


def _t5_bucket(distance):
    small = distance < MAX_EXACT
    nf = jnp.maximum(distance, 1).astype(F32)
    large = MAX_EXACT + (jnp.log(nf / MAX_EXACT) / math.log(REL_MAX_DISTANCE / MAX_EXACT)
                         * (N_BUCKETS - MAX_EXACT)).astype(jnp.int32)
    large = jnp.minimum(large, N_BUCKETS - 1)
    return jnp.where(small, distance, large)


def _prep_kernel(rb_ref, bucket_ref, *refs, n_cast):
    cast_src, tables_ref, cast_dst = refs[:n_cast], refs[n_cast], refs[n_cast + 1:]
    _cast_riders(cast_src, cast_dst)
    i_idx = lax.broadcasted_iota(jnp.int32, (BLK, 2 * BLK), 0)
    j_idx = lax.broadcasted_iota(jnp.int32, (BLK, 2 * BLK), 1)
    rel = BLK + i_idx - j_idx
    band = (rel >= 0) & (rel <= BLK)
    for k in range(PREP_HEADS_PER_STEP):
        gh = pl.program_id(0) * PREP_HEADS_PER_STEP + k
        g, h = gh // ATT_HEADS, gh % ATT_HEADS
        bucket = bucket_ref[g]
        acc = jnp.zeros((BLK, 2 * BLK), F32)
        for b in range(N_BUCKETS):
            acc = jnp.where(bucket == b, rb_ref[b, gh], acc)
        pair = h // HEADS_PER_VREG
        rows = pl.ds(pl.multiple_of((h % HEADS_PER_VREG) * BLK, BLK), BLK)
        tables_ref[g, 0, pair, rows, :] = jnp.where(band, acc * LOG2E, NEG_INF)
        tables_ref[g, 1, pair, rows, :] = jnp.where(band & (j_idx >= BLK), acc * LOG2E, NEG_INF)


def _prep(rel_bias, to_cast):
    n_steps = N_DIL_GROUPS * ATT_HEADS // PREP_HEADS_PER_STEP
    rel = BLK + jnp.arange(BLK)[:, None] - jnp.arange(2 * BLK)[None, :]
    buckets = jnp.stack([_t5_bucket(jnp.clip(rel, 0, BLK) * dil) for _, dil in WINDOW_DILATIONS])
    shape = (N_DIL_GROUPS, 2, HEAD_PAIRS, HEADS_PER_VREG * BLK, 2 * BLK)
    cast_in, cast_out, cast_shape = _cast_rider_specs(to_cast, n_steps)
    return pl.pallas_call(
        functools.partial(_prep_kernel, n_cast=len(to_cast)),
        grid=(n_steps,),
        in_specs=[pl.BlockSpec(memory_space=pltpu.SMEM), _whole(buckets.shape)] + cast_in,
        out_specs=[_whole(shape)] + cast_out,
        out_shape=[jax.ShapeDtypeStruct(shape, F32)] + cast_shape,
        compiler_params=pltpu.CompilerParams(dimension_semantics=("arbitrary",),
                                             vmem_limit_bytes=VMEM_LIMIT),
        name="bias_tables_and_casts",
    )(rel_bias.astype(F32), buckets.astype(jnp.int32), *[arr for arr, _ in to_cast])


def _lse_lane(head):
    return (head % HEADS_PER_VREG) * HEAD_DIM + HEADS_PER_VREG * (head // HEADS_PER_VREG)


def _attn_group_body(in_refs, tb_ref, out_refs, whole_sequence, seq_step):
    if whole_sequence:
        q_ref, k_ref, v_ref = in_refs
    else:
        q_ref, k_ref, v_ref, kp_ref, vp_ref = in_refs
    o_ref, lse_ref = out_refs
    rows = q_ref.shape[0]

    def unit(q, keys, vals, variant):
        lane = lax.broadcasted_iota(jnp.int32, (BLK, LANES), 1)
        first = lane < HEAD_DIM
        ones = jnp.ones((2 * BLK, LANES), BF16)
        outs = []
        lse = jnp.zeros((BLK, LANES), F32)
        for pair in range(HEAD_PAIRS):
            sl = slice(pair * LANES, (pair + 1) * LANES)
            qp, kp, vp = q[:, sl], keys[:, sl], vals[:, sl]
            zero = jnp.zeros_like(qp)
            qq = jnp.concatenate([jnp.where(first, qp, zero), jnp.where(first, zero, qp)], axis=0)
            logits = _dot_nt(qq, kp) + tb_ref[variant, pair]
            m = jnp.max(logits, axis=-1, keepdims=True)
            p = jnp.exp2((logits - m).astype(BF16))
            res = _dot(p, jnp.concatenate([vp, ones], axis=1))
            num = jnp.where(first, res[:BLK, :LANES], res[BLK:, :LANES])
            den = jnp.where(first, res[:BLK, LANES:], res[BLK:, LANES:])
            outs.append((num / den).astype(BF16))
            lse_pair = jnp.where(first, m[:BLK], m[BLK:]) + jnp.log2(den)
            keep = (lane == _lse_lane(pair * HEADS_PER_VREG)) | (lane == _lse_lane(pair * HEADS_PER_VREG + 1))
            lse = jnp.where(keep, lse_pair, lse)
        return jnp.concatenate(outs, axis=1), lse

    first_variant = 1 if whole_sequence else jnp.where(seq_step == 0, 1, 0)
    for s in range(q_ref.shape[1] // ATT_WIDTH):
        cols = slice(s * ATT_WIDTH, (s + 1) * ATT_WIDTH)
        lse_cols = slice(s * LANES, (s + 1) * LANES)
        for u in range(rows // BLK):
            cur_rows = slice(u * BLK, (u + 1) * BLK)
            if u > 0:
                both = slice((u - 1) * BLK, (u + 1) * BLK)
                o, lse = unit(q_ref[cur_rows, cols], k_ref[both, cols], v_ref[both, cols], 0)
            else:
                k_before = k_ref[cur_rows, cols] if whole_sequence else kp_ref[:, cols]
                v_before = v_ref[cur_rows, cols] if whole_sequence else vp_ref[:, cols]
                keys = jnp.concatenate([k_before, k_ref[cur_rows, cols]], axis=0)
                vals = jnp.concatenate([v_before, v_ref[cur_rows, cols]], axis=0)
                o, lse = unit(q_ref[cur_rows, cols], keys, vals, first_variant)
            o_ref[cur_rows, cols] = o
            lse_ref[cur_rows, lse_cols] = lse


def _attn_kernel(*refs, whole_flags):
    pos, ins = 0, []
    for whole in whole_flags:
        n = 3 if whole else 5
        ins.append(refs[pos:pos + n])
        pos += n
    tables_ref = refs[pos]
    outs = refs[pos + 1:]
    for g, whole in enumerate(whole_flags):
        _attn_group_body(ins[g], tables_ref.at[g], outs[2 * g:2 * g + 2], whole, pl.program_id(1))


def _attention(qkv, tables, batch, seq):
    w = ATT_WIDTH
    operands, in_specs, out_specs, out_shape, whole_flags, inner = [], [], [], [], [], None
    for g, (_, dil) in enumerate(WINDOW_DILATIONS):
        sub = seq // dil
        rows = min(ATTN_BLOCKS_PER_STEP * BLK, sub)
        n_seq = ATTN_BLOCKS_PER_STEP * BLK // rows
        assert sub % rows == 0 and dil % n_seq == 0
        whole = rows == sub
        steps = (dil // n_seq) * (sub // rows)
        assert (whole or dil == n_seq) and inner in (None, steps)
        inner = steps
        q, k, v = (qkv[part * N_DIL_GROUPS + g].reshape(batch, sub, dil * w) for part in range(3))
        if whole:
            cur = pl.BlockSpec((None, rows, n_seq * w), lambda b, j: (b, 0, j))
            lse_spec = pl.BlockSpec((None, rows, n_seq * LANES), lambda b, j: (b, 0, j))
            operands += [q, k, v]
            in_specs += [cur, cur, cur]
        else:
            bps = rows // BLK
            cur = pl.BlockSpec((None, rows, n_seq * w), lambda b, j: (b, j, 0))
            lse_spec = pl.BlockSpec((None, rows, n_seq * LANES), lambda b, j: (b, j, 0))
            prev = pl.BlockSpec((None, BLK, n_seq * w), lambda b, j, bps=bps: (b, jnp.maximum(j * bps - 1, 0), 0))
            operands += [q, k, v, k, v]
            in_specs += [cur, cur, cur, prev, prev]
        whole_flags.append(whole)
        out_specs += [cur, lse_spec]
        out_shape += [jax.ShapeDtypeStruct((batch, sub, dil * w), BF16),
                      jax.ShapeDtypeStruct((batch, sub, dil * LANES), F32)]
    res = pl.pallas_call(
        functools.partial(_attn_kernel, whole_flags=tuple(whole_flags)),
        grid=(batch, inner),
        in_specs=in_specs + [_whole(tables.shape)],
        out_specs=out_specs,
        out_shape=out_shape,
        compiler_params=pltpu.CompilerParams(dimension_semantics=("arbitrary", "arbitrary"),
                                             vmem_limit_bytes=VMEM_LIMIT),
        name="dilated_attention",
    )(*operands, tables)
    os_ = [o.reshape(-1, o.shape[-1]) for o in res[0::2]]
    lses = [l.reshape(-1, l.shape[-1]) for l in res[1::2]]
    return os_, lses


def _natural_order(ref, scr_ref, sub, dil, width):
    per = SUB_TILE // dil
    rows = slice(sub * per, (sub + 1) * per)
    if dil == 1:
        return ref[rows, :].astype(F32)
    n_col = width // LANES
    for r in range(dil):
        for c in range(n_col):
            lanes = slice(r * width + c * LANES, r * width + (c + 1) * LANES)
            scr_ref[c, pl.ds(r, per, stride=dil), :] = ref[rows, lanes].astype(F32)
    return jnp.concatenate([scr_ref[c] for c in range(n_col)], axis=1)


def _layer1_tail_chain(sub, expand, h_ref, o_refs, l_refs, wout_ref, mlpg_ref, wup_ref, wdn_ref,
                       fing_ref, out_ref, os_ref, ls_ref):
    rows = slice(sub * SUB_TILE, (sub + 1) * SUB_TILE)
    dils = [d for _, d in WINDOW_DILATIONS]
    lses = [_natural_order(l_ref, None if g == 0 else ls_ref.at[sub, g - 1], sub, dils[g], LANES)
            for g, l_ref in enumerate(l_refs)]
    outs = [_natural_order(o_ref, None if g == 0 else os_ref.at[sub, g - 1], sub, dils[g], ATT_WIDTH)
            for g, o_ref in enumerate(o_refs)]
    m_all = jnp.maximum(jnp.maximum(lses[0], lses[1]), lses[2])
    es = [jnp.exp2(l - m_all) for l in lses]
    inv = 1.0 / (es[0] + es[1] + es[2])
    o = outs[-1]
    for e, og in zip(es[:-1], outs[:-1]):
        o = o + _dot((e * inv).astype(BF16), expand) * (og - outs[-1])
    yield
    h1 = h_ref[rows, :] + _dot(o.astype(BF16), wout_ref[...])
    yield
    h2 = yield from _mlp_residual(h1, mlpg_ref[...], wup_ref, wdn_ref)
    out_ref[rows, :] = _rms_norm(h2, fing_ref[...])


def _layer1_tail_kernel(h_ref, o0_ref, o1_ref, o2_ref, l0_ref, l1_ref, l2_ref, *refs):
    row = lax.broadcasted_iota(jnp.int32, (LANES, ATT_WIDTH), 0)
    head = lax.broadcasted_iota(jnp.int32, (LANES, ATT_WIDTH), 1) // HEAD_DIM
    src = (head % HEADS_PER_VREG) * HEAD_DIM + HEADS_PER_VREG * (head // HEADS_PER_VREG)
    expand = jnp.where(row == src, 1.0, 0.0).astype(BF16)
    _interleave((_layer1_tail_chain(sub, expand, h_ref, (o0_ref, o1_ref, o2_ref), (l0_ref, l1_ref, l2_ref), *refs)
                 for sub in range(h_ref.shape[0] // SUB_TILE)), CHAIN_LAG)


def _layer1_tail(h, os_, lses, w_out, mlp_g, w_up, w_down, fin_g):
    t = h.shape[0]
    tile = pl.BlockSpec((TOKEN_TILE, D_MODEL), lambda i: (i, 0))
    otiles = [pl.BlockSpec((TOKEN_TILE // d, d * ATT_WIDTH), lambda i: (i, 0)) for _, d in WINDOW_DILATIONS]
    ltiles = [pl.BlockSpec((TOKEN_TILE // d, d * LANES), lambda i: (i, 0)) for _, d in WINDOW_DILATIONS]
    return pl.pallas_call(
        _layer1_tail_kernel,
        grid=(t // TOKEN_TILE,),
        in_specs=[tile, *otiles, *ltiles, _whole(w_out.shape),
                  _whole((1, D_MODEL)), _whole(w_up.shape), _whole(w_down.shape),
                  _whole((1, D_MODEL))],
        out_specs=tile,
        out_shape=jax.ShapeDtypeStruct((t, D_MODEL), F32),
        scratch_shapes=[
            pltpu.VMEM((TOKEN_TILE // SUB_TILE, N_DIL_GROUPS - 1, ATT_WIDTH // LANES, SUB_TILE, LANES), F32),
            pltpu.VMEM((TOKEN_TILE // SUB_TILE, N_DIL_GROUPS - 1, 1, SUB_TILE, LANES), F32)],
        compiler_params=pltpu.CompilerParams(dimension_semantics=("arbitrary",),
                                             vmem_limit_bytes=VMEM_LIMIT),
        name="layer1_merge_proj_mlp",
    )(h, *os_, *lses, w_out, mlp_g, w_up, w_down, fin_g)


def kernel(x, mix_norm_g, mlp_norm_g, final_norm_g, a_w_in, a_ln_g, a_ln_b, a_w_s, a_b_s, a_w_out,
           b_w_qkv, b_w_out, rel_bias, w_up, w_down):
    batch, seq, d = x.shape
    max_dil = WINDOW_DILATIONS[-1][1]
    assert d == D_MODEL and seq % (BLK * max_dil) == 0
    assert seq % TOKEN_TILE == 0 and TOKEN_TILE % SUB_TILE == 0
    assert SUB_TILE % (16 * max_dil) == 0 and TOKEN_TILE % LAYER0_SUB_TILE == 0 and LAYER0_SUB_TILE % CHUNK == 0
    row = lambda a: a.reshape(1, -1).astype(F32)
    h = x.reshape(batch * seq, d)

    tables, w_in0, w_out0, w_up0, w_down0 = _prep(rel_bias, [(a_w_in, 0), (a_w_out, 0), (w_up, 0), (w_down, 0)])
    h, w_qkv = _layer0(h, row(mix_norm_g[0]), w_in0, row(a_ln_g[0]), row(a_ln_b[0]), a_w_s[0], a_b_s[0].T,
                       w_out0, row(mlp_norm_g[0]), w_up0, w_down0, (b_w_qkv, 0))

    *qkv, w_up1, w_down1, w_out1 = _qkv(h, row(mix_norm_g[1]), w_qkv, [(w_up, 1), (w_down, 1), (b_w_out, 0)])
    os_, lses = _attention(qkv, tables, batch, seq)

    out = _layer1_tail(h, os_, lses, w_out1, row(mlp_norm_g[1]), w_up1, w_down1, row(final_norm_g))
    return out.reshape(batch, seq, d)
```
